```python
import jax, jax.numpy as jnp
from jax import lax
import numpy as np

D_MODEL = 1024
BATCH = 16
SEQ = 2048
DEPTH = 2

N_MIXERS = 2
N_A_LAYERS = (DEPTH + 1) // 2
N_B_LAYERS = DEPTH // 2
D_FF = ((8 * D_MODEL // 3 + 127) // 128) * 128
CHUNK = 128
A_DIM = 2 * D_MODEL
A_HEADS = 8
A_HEAD_DIM = A_DIM // A_HEADS
POOL_WINDOWS = (2, 4, 8, 16)
B_GROUPS = len(POOL_WINDOWS)
B_DIM = D_MODEL
B_GROUP_DIM = B_DIM // B_GROUPS
EPS = 1e-6

kernel_name = "hybrid_sgu_pool_macaron"


def rmsnorm(x, g):
    xf = x.astype(jnp.float32)
    y = xf * lax.rsqrt(jnp.mean(xf * xf, axis=-1, keepdims=True) + EPS)
    return (y * g.astype(jnp.float32)).astype(x.dtype)


def swiglu(h, w_in, w_out):
    gate, up = jnp.split(h @ w_in, 2, axis=-1)
    return (jax.nn.silu(gate) * up) @ w_out


def chunked_sgu(h, w_in, v_norm, w_s, b_s, w_out):
    bsz, seq, _ = h.shape
    z = jax.nn.gelu(h @ w_in)
    u, v = jnp.split(z, 2, axis=-1)
    v = rmsnorm(v, v_norm)
    v = v.reshape(bsz, seq // CHUNK, CHUNK, A_HEADS, A_HEAD_DIM)
    causal = jnp.tril(jnp.ones((CHUNK, CHUNK), dtype=bool))
    ws = jnp.where(causal[None], w_s, jnp.zeros_like(w_s))
    gate = jnp.einsum('hts,bcshd->bcthd', ws, v) + b_s.T[None, None, :, :, None]
    gate = gate.reshape(bsz, seq, A_DIM)
    return (u * gate) @ w_out


def causal_mean(x, window):
    seq = x.shape[1]
    c = jnp.cumsum(x.astype(jnp.float32), axis=1)
    c_prev = jnp.pad(c, ((0, 0), (window, 0), (0, 0)))[:, :seq]
    cnt = jnp.minimum(jnp.arange(1, seq + 1), window).astype(jnp.float32)[None, :, None]
    return ((c - c_prev) / cnt).astype(x.dtype)


def pool_mixer(h, w_in, w_grp, scale, w_out):
    bsz, seq, _ = h.shape
    p = h @ w_in
    groups = jnp.split(p, B_GROUPS, axis=-1)
    pooled = jnp.stack([causal_mean(g, w) - g for g, w in zip(groups, POOL_WINDOWS)], axis=2)
    y = jnp.einsum('bsgc,gcd->bsgd', pooled, w_grp).reshape(bsz, seq, B_DIM)
    return (y * scale) @ w_out


def setup_inputs(seed: int = 0) -> dict:
    key = jax.random.key(seed)
    ks = jax.random.split(key, 16)
    f32 = jnp.float32
    nrm = lambda k, shape, fan_in: jax.random.normal(k, shape, f32) * (fan_in ** -0.5)
    gain = lambda k, shape: 1.0 + 0.02 * jax.random.normal(k, shape, f32)
    return {
        "x": jax.random.normal(ks[0], (BATCH, SEQ, D_MODEL), f32),
        "ffn_norm": gain(ks[1], (DEPTH, 2, D_MODEL)),
        "ffn_w_in": nrm(ks[2], (DEPTH, 2, D_MODEL, 2 * D_FF), D_MODEL),
        "ffn_w_out": nrm(ks[3], (DEPTH, 2, D_FF, D_MODEL), D_FF),
        "mix_norm": gain(ks[4], (DEPTH, D_MODEL)),
        "a_w_in": nrm(ks[5], (N_A_LAYERS, D_MODEL, 2 * A_DIM), D_MODEL),
        "a_v_norm": gain(ks[6], (N_A_LAYERS, A_DIM)),
        "a_w_s": nrm(ks[7], (N_A_LAYERS, A_HEADS, CHUNK, CHUNK), CHUNK),
        "a_b_s": 1.0 + 0.02 * jax.random.normal(ks[8], (N_A_LAYERS, A_HEADS, CHUNK), f32),
        "a_w_out": nrm(ks[9], (N_A_LAYERS, A_DIM, D_MODEL), A_DIM),
        "b_w_in": nrm(ks[10], (N_B_LAYERS, D_MODEL, B_DIM), D_MODEL),
        "b_w_grp": nrm(ks[11], (N_B_LAYERS, B_GROUPS, B_GROUP_DIM, B_GROUP_DIM), B_GROUP_DIM),
        "b_scale": gain(ks[12], (N_B_LAYERS, B_DIM)),
        "b_w_out": nrm(ks[13], (N_B_LAYERS, B_DIM, D_MODEL), B_DIM),
        "final_norm": gain(ks[14], (D_MODEL,)),
    }


def reference(x, ffn_norm, ffn_w_in, ffn_w_out, mix_norm, a_w_in, a_v_norm, a_w_s, a_b_s,
              a_w_out, b_w_in, b_w_grp, b_scale, b_w_out, final_norm):
    for i in range(DEPTH):
        x = x + 0.5 * swiglu(rmsnorm(x, ffn_norm[i, 0]), ffn_w_in[i, 0], ffn_w_out[i, 0])
        h = rmsnorm(x, mix_norm[i])
        j = i // N_MIXERS
        if i % N_MIXERS == 0:
            x = x + chunked_sgu(h, a_w_in[j], a_v_norm[j], a_w_s[j], a_b_s[j], a_w_out[j])
        else:
            x = x + pool_mixer(h, b_w_in[j], b_w_grp[j], b_scale[j], b_w_out[j])
        x = x + 0.5 * swiglu(rmsnorm(x, ffn_norm[i, 1]), ffn_w_in[i, 1], ffn_w_out[i, 1])
    return rmsnorm(x, final_norm)
```

```python
import functools

import jax
import jax.numpy as jnp
from jax import lax
from jax.experimental import pallas as pl
from jax.experimental.pallas import tpu as pltpu

D_MODEL = 1024
D_FF = 2816
CHUNK = 128
A_DIM = 2 * D_MODEL
A_HEADS = 8
A_HEAD_DIM = A_DIM // A_HEADS
POOL_WINDOWS = (2, 4, 8, 16)
B_GROUPS = len(POOL_WINDOWS)
B_GROUP_DIM = D_MODEL // B_GROUPS
EPS = 1e-6

HALO = 16
MXU_N = 256
VMEM_LIMIT = 56 * 1024 * 1024

TM = 512


def _rmsnorm(x, g):
    ms = jnp.mean(x * x, axis=-1, keepdims=True)
    return x * lax.rsqrt(ms + EPS) * g


def _dot(a, b):
    return jnp.dot(a, b, preferred_element_type=jnp.float32)


def _resident(shape):
    return pl.BlockSpec(shape, lambda *_: (0,) * len(shape),
                        pipeline_mode=pl.Buffered(1))


def _ffn_kernel(x_ref, g_ref, win_ref, wout_ref, fg_ref, o_ref, h_ref, a_ref,
                *, ff_chunk, final_norm):
    h_ref[...] = _rmsnorm(x_ref[...], g_ref[...]).astype(jnp.bfloat16)
    for c in range(D_FF // ff_chunk):
        lo = c * ff_chunk
        gate = _dot(h_ref[...], win_ref[:, lo:lo + ff_chunk])
        up = _dot(h_ref[...], win_ref[:, D_FF + lo:D_FF + lo + ff_chunk])
        a_ref[:, lo:lo + ff_chunk] = (
            gate * jax.nn.sigmoid(gate) * up).astype(jnp.bfloat16)
    y = x_ref[...] + 0.5 * _dot(a_ref[...], wout_ref[...])
    if final_norm:
        y = _rmsnorm(y, fg_ref[...])
    o_ref[...] = y


def _ffn(x2, g, w_in, w_out, fg, *, final_norm):
    t, d = x2.shape
    kern = functools.partial(_ffn_kernel, ff_chunk=MXU_N, final_norm=final_norm)
    return pl.pallas_call(
        kern,
        grid=(t // TM,),
        in_specs=[
            pl.BlockSpec((TM, d), lambda i: (i, 0)),
            _resident((1, d)),
            _resident(w_in.shape),
            _resident(w_out.shape),
            _resident((1, d)),
        ],
        out_specs=pl.BlockSpec((TM, d), lambda i: (i, 0)),
        out_shape=jax.ShapeDtypeStruct((t, d), jnp.float32),
        scratch_shapes=[
            pltpu.VMEM((TM, d), jnp.bfloat16),
            pltpu.VMEM((TM, D_FF), jnp.bfloat16),
        ],
        compiler_params=pltpu.CompilerParams(
            dimension_semantics=("arbitrary",), vmem_limit_bytes=VMEM_LIMIT),
        name="ffn",
    )(x2, g, w_in, w_out, fg)


def _gelu_tanh(x):
    c = 0.7978845608028654
    return 0.5 * x * (1.0 + jnp.tanh(c * (x + 0.044715 * (x * x * x))))


def _mixa_kernel(x_ref, g_ref, win_ref, vn_ref, ws_ref, bst_ref, wout_ref,
                 o_ref, h_ref, u_ref, v_ref, ug_ref, *, col_chunk):
    tm = x_ref.shape[0]
    h_ref[...] = _rmsnorm(x_ref[...], g_ref[...]).astype(jnp.bfloat16)
    ssq = jnp.zeros((tm, 1), jnp.float32)
    for c in range(A_DIM // col_chunk):
        lo = c * col_chunk
        u_ref[:, lo:lo + col_chunk] = _gelu_tanh(
            _dot(h_ref[...], win_ref[:, lo:lo + col_chunk]))
        v = _gelu_tanh(_dot(h_ref[...], win_ref[:, A_DIM + lo:A_DIM + lo + col_chunk]))
        ssq = ssq + jnp.sum(v * v, axis=-1, keepdims=True)
        v_ref[:, lo:lo + col_chunk] = v
    inv = lax.rsqrt(ssq * (1.0 / A_DIM) + EPS)
    row = lax.broadcasted_iota(jnp.int32, (CHUNK, CHUNK), 0)
    col = lax.broadcasted_iota(jnp.int32, (CHUNK, CHUNK), 1)
    causal = col <= row
    for hd in range(A_HEADS):
        lo = hd * A_HEAD_DIM
        ws = jnp.where(causal, ws_ref[hd], 0.0).astype(jnp.bfloat16)
        bias = bst_ref[:, hd:hd + 1]
        vn = (v_ref[:, lo:lo + A_HEAD_DIM] * inv
              * vn_ref[:, lo:lo + A_HEAD_DIM]).astype(jnp.bfloat16)
        for ck in range(tm // CHUNK):
            r0 = ck * CHUNK
            gate = _dot(ws, vn[r0:r0 + CHUNK, :]) + bias
            ug_ref[r0:r0 + CHUNK, lo:lo + A_HEAD_DIM] = (
                u_ref[r0:r0 + CHUNK, lo:lo + A_HEAD_DIM] * gate
            ).astype(jnp.bfloat16)
    o_ref[...] = x_ref[...] + _dot(ug_ref[...], wout_ref[...])


def _mixa(x2, g, w_in, v_norm, w_s, b_st, w_out):
    t, d = x2.shape
    kern = functools.partial(_mixa_kernel, col_chunk=2 * MXU_N)
    return pl.pallas_call(
        kern,
        grid=(t // TM,),
        in_specs=[
            pl.BlockSpec((TM, d), lambda i: (i, 0)),
            _resident((1, d)),
            _resident(w_in.shape),
            _resident((1, A_DIM)),
            _resident(w_s.shape),
            _resident(b_st.shape),
            _resident(w_out.shape),
        ],
        out_specs=pl.BlockSpec((TM, d), lambda i: (i, 0)),
        out_shape=jax.ShapeDtypeStruct((t, d), jnp.float32),
        scratch_shapes=[
            pltpu.VMEM((TM, d), jnp.bfloat16),
            pltpu.VMEM((TM, A_DIM), jnp.float32),
            pltpu.VMEM((TM, A_DIM), jnp.float32),
            pltpu.VMEM((TM, A_DIM), jnp.bfloat16),
        ],
        compiler_params=pltpu.CompilerParams(
            dimension_semantics=("arbitrary",), vmem_limit_bytes=VMEM_LIMIT),
        name="mix_sgu",
    )(x2, g, w_in, v_norm, w_s, b_st, w_out)


def _mixb_kernel(x_ref, g_ref, win_ref, wgrp_ref, scale_ref, wout_ref, o_ref,
                 ext_ref, y_ref):
    tm = x_ref.shape[0]
    s = pl.program_id(1)

    @pl.when(s == 0)
    def _():
        ext_ref[0:HALO, :] = jnp.zeros((HALO, D_MODEL), jnp.float32)

    h = _rmsnorm(x_ref[...], g_ref[...]).astype(jnp.bfloat16)
    ext_ref[HALO:, :] = _dot(h, win_ref[...])
    pos = s * tm + lax.broadcasted_iota(jnp.int32, (tm, 1), 0)
    for gi, w in enumerate(POOL_WINDOWS):
        lo = gi * B_GROUP_DIM
        acc = ext_ref[:, lo:lo + B_GROUP_DIM]
        k = 1
        while k < w:
            acc = acc + pltpu.roll(acc, k, axis=0)
            k *= 2
        cnt = jnp.minimum(pos + 1, w).astype(jnp.float32)
        pooled = acc[HALO:, :] / cnt - ext_ref[HALO:, lo:lo + B_GROUP_DIM]
        y = _dot(pooled.astype(jnp.bfloat16), wgrp_ref[gi])
        y_ref[:, lo:lo + B_GROUP_DIM] = (
            y * scale_ref[:, lo:lo + B_GROUP_DIM]).astype(jnp.bfloat16)
    ext_ref[0:HALO, :] = ext_ref[tm:tm + HALO, :]
    o_ref[...] = x_ref[...] + _dot(y_ref[...], wout_ref[...])


def _mixb(x3, g, w_in, w_grp, scale, w_out):
    b, s, d = x3.shape
    return pl.pallas_call(
        _mixb_kernel,
        grid=(b, s // TM),
        in_specs=[
            pl.BlockSpec((None, TM, d), lambda i, j: (i, j, 0)),
            _resident((1, d)),
            _resident(w_in.shape),
            _resident(w_grp.shape),
            _resident((1, d)),
            _resident(w_out.shape),
        ],
        out_specs=pl.BlockSpec((None, TM, d), lambda i, j: (i, j, 0)),
        out_shape=jax.ShapeDtypeStruct((b, s, d), jnp.float32),
        scratch_shapes=[
            pltpu.VMEM((HALO + TM, d), jnp.float32),
            pltpu.VMEM((TM, d), jnp.bfloat16),
        ],
        compiler_params=pltpu.CompilerParams(
            dimension_semantics=("arbitrary", "arbitrary"),
            vmem_limit_bytes=VMEM_LIMIT),
        name="mix_pool",
    )(x3, g, w_in, w_grp, scale, w_out)


def kernel(x, ffn_norm, ffn_w_in, ffn_w_out, mix_norm, a_w_in, a_v_norm, a_w_s,
           a_b_s, a_w_out, b_w_in, b_w_grp, b_scale, b_w_out, final_norm):
    bsz, seq, d = x.shape
    depth = ffn_norm.shape[0]
    bf = lambda w: w.astype(jnp.bfloat16)
    row = lambda v: v.reshape(1, -1)
    x2 = x.reshape(bsz * seq, d)
    fg = row(final_norm)
    for i in range(depth):
        x2 = _ffn(x2, row(ffn_norm[i, 0]), bf(ffn_w_in[i, 0]), bf(ffn_w_out[i, 0]),
                  fg, final_norm=False)
        j = i // 2
        if i % 2 == 0:
            x2 = _mixa(x2, row(mix_norm[i]), bf(a_w_in[j]), row(a_v_norm[j]),
                       a_w_s[j], a_b_s[j].T, bf(a_w_out[j]))
        else:
            x3 = _mixb(x2.reshape(bsz, seq, d), row(mix_norm[i]), bf(b_w_in[j]),
                       bf(b_w_grp[j]), row(b_scale[j]), bf(b_w_out[j]))
            x2 = x3.reshape(bsz * seq, d)
        x2 = _ffn(x2, row(ffn_norm[i, 1]), bf(ffn_w_in[i, 1]), bf(ffn_w_out[i, 1]),
                  fg, final_norm=(i == depth - 1))
    return x2.reshape(bsz, seq, d)
```

```python
import functools

import jax
import jax.numpy as jnp
from jax import lax
from jax.experimental import pallas as pl
from jax.experimental.pallas import tpu as pltpu

D_MODEL = 1024
D_FF = 2816
CHUNK = 128
A_DIM = 2 * D_MODEL
A_HEADS = 8
A_HEAD_DIM = A_DIM // A_HEADS
POOL_WINDOWS = (2, 4, 8, 16)
B_GROUPS = len(POOL_WINDOWS)
B_GROUP_DIM = D_MODEL // B_GROUPS
EPS = 1e-6

HALO = 16
MXU_N = 256
BF16_SUBLANES = 16
VMEM_LIMIT = 56 * 1024 * 1024

TM = 512
FFN_SUBTILES = 2


def _rmsnorm(x, g):
    ms = jnp.mean(x * x, axis=-1, keepdims=True)
    return x * lax.rsqrt(ms + EPS) * g


def _dot(a, b):
    return jnp.dot(a, b, preferred_element_type=jnp.float32)


def _resident(shape):
    return pl.BlockSpec(shape, lambda *_: (0,) * len(shape),
                        pipeline_mode=pl.Buffered(1))


def _cast_blocks(rows, n_steps):
    best = 1
    for nb in range(1, n_steps + 1):
        if rows % nb == 0 and (rows // nb) % BF16_SUBLANES == 0:
            best = nb
    return best


def _cast_specs(casts, n_steps, step_of):
    in_specs, out_specs, out_shapes = [], [], []
    for w, layer in casts:
        _, rows, cols = w.shape
        nb = _cast_blocks(rows, n_steps)
        rb = rows // nb

        def in_map(*ids, layer=layer, nb=nb):
            return (layer, jnp.minimum(step_of(*ids), nb - 1), 0)

        def out_map(*ids, nb=nb):
            return (jnp.minimum(step_of(*ids), nb - 1), 0)

        in_specs.append(pl.BlockSpec((None, rb, cols), in_map))
        out_specs.append(pl.BlockSpec((rb, cols), out_map))
        out_shapes.append(jax.ShapeDtypeStruct((rows, cols), jnp.bfloat16))
    return in_specs, out_specs, out_shapes


def _run_casts(src_refs, dst_refs):
    for src, dst in zip(src_refs, dst_refs):
        dst[...] = src[...].astype(jnp.bfloat16)


def _split_refs(refs, n_in, n_cast):
    ins = refs[:n_in]
    cast_src = refs[n_in:n_in + n_cast]
    out = refs[n_in + n_cast]
    cast_dst = refs[n_in + n_cast + 1:n_in + 2 * n_cast + 1]
    scratch = refs[n_in + 2 * n_cast + 1:]
    return ins, cast_src, out, cast_dst, scratch


def _ffn_kernel(*refs, n_cast, ff_chunk, final_norm):
    (x_ref, g_ref, win_ref, wout_ref, fg_ref), cast_src, o_ref, cast_dst, (
        h_ref, a_ref) = _split_refs(refs, 5, n_cast)
    _run_casts(cast_src, cast_dst)
    for r0 in range(0, x_ref.shape[0], TM):
        rows = pl.ds(r0, TM)
        h_ref[rows, :] = _rmsnorm(x_ref[rows, :], g_ref[...]).astype(jnp.bfloat16)
        for c in range(D_FF // ff_chunk):
            lo = c * ff_chunk
            gate = _dot(h_ref[rows, :], win_ref[:, lo:lo + ff_chunk])
            up = _dot(h_ref[rows, :], win_ref[:, D_FF + lo:D_FF + lo + ff_chunk])
            a_ref[rows, lo:lo + ff_chunk] = (
                gate * jax.nn.sigmoid(gate) * up).astype(jnp.bfloat16)
        y = x_ref[rows, :] + 0.5 * _dot(a_ref[rows, :], wout_ref[...])
        if final_norm:
            y = _rmsnorm(y, fg_ref[...])
        o_ref[rows, :] = y


def _ffn(x2, g, w_in, w_out, fg, casts, *, final_norm):
    t, d = x2.shape
    tb = FFN_SUBTILES * TM
    n_steps = t // tb
    c_in, c_out, c_shapes = _cast_specs(casts, n_steps, lambda i: i)
    kern = functools.partial(_ffn_kernel, n_cast=len(casts), ff_chunk=MXU_N,
                             final_norm=final_norm)
    outs = pl.pallas_call(
        kern,
        grid=(n_steps,),
        in_specs=[
            pl.BlockSpec((tb, d), lambda i: (i, 0)),
            _resident((1, d)),
            _resident(w_in.shape),
            _resident(w_out.shape),
            _resident((1, d)),
        ] + c_in,
        out_specs=[pl.BlockSpec((tb, d), lambda i: (i, 0))] + c_out,
        out_shape=[jax.ShapeDtypeStruct((t, d), jnp.float32)] + c_shapes,
        scratch_shapes=[
            pltpu.VMEM((tb, d), jnp.bfloat16),
            pltpu.VMEM((tb, D_FF), jnp.bfloat16),
        ],
        compiler_params=pltpu.CompilerParams(
            dimension_semantics=("arbitrary",), vmem_limit_bytes=VMEM_LIMIT),
        name="ffn",
    )(x2, g, w_in, w_out, fg, *[w for w, _ in casts])
    return outs[0], outs[1:]


def _gelu_tanh(x):
    c = 0.7978845608028654
    return 0.5 * x * (1.0 + jnp.tanh(c * (x + 0.044715 * (x * x * x))))


def _mixa_kernel(*refs, n_cast, col_chunk):
    (x_ref, g_ref, win_ref, vn_ref, ws_ref, bst_ref, wout_ref), cast_src, o_ref, \
        cast_dst, (h_ref, u_ref, v_ref, ug_ref) = _split_refs(refs, 7, n_cast)
    _run_casts(cast_src, cast_dst)
    tm = x_ref.shape[0]
    h_ref[...] = _rmsnorm(x_ref[...], g_ref[...]).astype(jnp.bfloat16)
    ssq = jnp.zeros((tm, 1), jnp.float32)
    for c in range(A_DIM // col_chunk):
        lo = c * col_chunk
        v = _gelu_tanh(_dot(h_ref[...], win_ref[:, A_DIM + lo:A_DIM + lo + col_chunk]))
        ssq = ssq + jnp.sum(v * v, axis=-1, keepdims=True)
        v_ref[:, lo:lo + col_chunk] = v
    inv = lax.rsqrt(ssq * (1.0 / A_DIM) + EPS)
    row = lax.broadcasted_iota(jnp.int32, (CHUNK, CHUNK), 0)
    col = lax.broadcasted_iota(jnp.int32, (CHUNK, CHUNK), 1)
    causal = col <= row
    for hd in range(A_HEADS):
        lo = hd * A_HEAD_DIM
        ws = jnp.where(causal, ws_ref[hd], 0.0).astype(jnp.bfloat16)
        bias = bst_ref[:, hd:hd + 1]
        vn = (v_ref[:, lo:lo + A_HEAD_DIM] * inv
              * vn_ref[:, lo:lo + A_HEAD_DIM]).astype(jnp.bfloat16)
        u_ref[:, lo:lo + A_HEAD_DIM] = _gelu_tanh(
            _dot(h_ref[...], win_ref[:, lo:lo + A_HEAD_DIM]))
        for ck in range(tm // CHUNK):
            r0 = ck * CHUNK
            gate = _dot(ws, vn[r0:r0 + CHUNK, :]) + bias
            ug_ref[r0:r0 + CHUNK, lo:lo + A_HEAD_DIM] = (
                u_ref[r0:r0 + CHUNK, lo:lo + A_HEAD_DIM] * gate
            ).astype(jnp.bfloat16)
    o_ref[...] = x_ref[...] + _dot(ug_ref[...], wout_ref[...])


def _mixa(x2, g, w_in, v_norm, w_s, b_st, w_out, casts):
    t, d = x2.shape
    n_steps = t // TM
    c_in, c_out, c_shapes = _cast_specs(casts, n_steps, lambda i: i)
    kern = functools.partial(_mixa_kernel, n_cast=len(casts), col_chunk=2 * MXU_N)
    outs = pl.pallas_call(
        kern,
        grid=(n_steps,),
        in_specs=[
            pl.BlockSpec((TM, d), lambda i: (i, 0)),
            _resident((1, d)),
            _resident(w_in.shape),
            _resident((1, A_DIM)),
            _resident(w_s.shape),
            _resident(b_st.shape),
            _resident(w_out.shape),
        ] + c_in,
        out_specs=[pl.BlockSpec((TM, d), lambda i: (i, 0))] + c_out,
        out_shape=[jax.ShapeDtypeStruct((t, d), jnp.float32)] + c_shapes,
        scratch_shapes=[
            pltpu.VMEM((TM, d), jnp.bfloat16),
            pltpu.VMEM((TM, A_DIM), jnp.float32),
            pltpu.VMEM((TM, A_DIM), jnp.float32),
            pltpu.VMEM((TM, A_DIM), jnp.bfloat16),
        ],
        compiler_params=pltpu.CompilerParams(
            dimension_semantics=("arbitrary",), vmem_limit_bytes=VMEM_LIMIT),
        name="mix_sgu",
    )(x2, g, w_in, v_norm, w_s, b_st, w_out, *[w for w, _ in casts])
    return outs[0], outs[1:]


def _mixb_kernel(*refs, n_cast):
    (x_ref, g_ref, win_ref, wgrp_ref, scale_ref, wout_ref), cast_src, o_ref, \
        cast_dst, (ext_ref, y_ref) = _split_refs(refs, 6, n_cast)
    _run_casts(cast_src, cast_dst)
    tm = x_ref.shape[0]
    s = pl.program_id(1)

    @pl.when(s == 0)
    def _():
        ext_ref[0:HALO, :] = jnp.zeros((HALO, D_MODEL), jnp.float32)

    h = _rmsnorm(x_ref[...], g_ref[...]).astype(jnp.bfloat16)
    ext_ref[HALO:, :] = _dot(h, win_ref[...])
    pos = s * tm + lax.broadcasted_iota(jnp.int32, (tm, 1), 0)
    for gi, w in enumerate(POOL_WINDOWS):
        lo = gi * B_GROUP_DIM
        acc = ext_ref[:, lo:lo + B_GROUP_DIM]
        k = 1
        while k < w:
            acc = acc + pltpu.roll(acc, k, axis=0)
            k *= 2
        cnt = jnp.minimum(pos + 1, w).astype(jnp.float32)
        pooled = acc[HALO:, :] / cnt - ext_ref[HALO:, lo:lo + B_GROUP_DIM]
        y = _dot(pooled.astype(jnp.bfloat16),
                 wgrp_ref[lo:lo + B_GROUP_DIM, :])
        y_ref[:, lo:lo + B_GROUP_DIM] = (
            y * scale_ref[:, lo:lo + B_GROUP_DIM]).astype(jnp.bfloat16)
    ext_ref[0:HALO, :] = ext_ref[tm:tm + HALO, :]
    o_ref[...] = x_ref[...] + _dot(y_ref[...], wout_ref[...])


def _mixb(x3, g, w_in, w_grp, scale, w_out, casts):
    b, s, d = x3.shape
    s_tiles = s // TM
    c_in, c_out, c_shapes = _cast_specs(casts, b * s_tiles,
                                        lambda i, j: i * s_tiles + j)
    kern = functools.partial(_mixb_kernel, n_cast=len(casts))
    outs = pl.pallas_call(
        kern,
        grid=(b, s_tiles),
        in_specs=[
            pl.BlockSpec((None, TM, d), lambda i, j: (i, j, 0)),
            _resident((1, d)),
            _resident(w_in.shape),
            _resident(w_grp.shape),
            _resident((1, d)),
            _resident(w_out.shape),
        ] + c_in,
        out_specs=[pl.BlockSpec((None, TM, d), lambda i, j: (i, j, 0))] + c_out,
        out_shape=[jax.ShapeDtypeStruct((b, s, d), jnp.float32)] + c_shapes,
        scratch_shapes=[
            pltpu.VMEM((HALO + TM, d), jnp.float32),
            pltpu.VMEM((TM, d), jnp.bfloat16),
        ],
        compiler_params=pltpu.CompilerParams(
            dimension_semantics=("arbitrary", "arbitrary"),
            vmem_limit_bytes=VMEM_LIMIT),
        name="mix_pool",
    )(x3, g, w_in, w_grp, scale, w_out, *[w for w, _ in casts])
    return outs[0], outs[1:]


def _stack3(w):
    return w.reshape((-1,) + w.shape[-2:])


def kernel(x, ffn_norm, ffn_w_in, ffn_w_out, mix_norm, a_w_in, a_v_norm, a_w_s,
           a_b_s, a_w_out, b_w_in, b_w_grp, b_scale, b_w_out, final_norm):
    bsz, seq, d = x.shape
    depth = ffn_norm.shape[0]
    row = lambda v: v.reshape(1, -1)
    fw_in, fw_out = _stack3(ffn_w_in), _stack3(ffn_w_out)
    bw_grp = b_w_grp.reshape(b_w_grp.shape[0], -1, b_w_grp.shape[-1])

    subs = []
    for i in range(depth):
        subs.append(("ffn", i, 0, [(fw_in, 2 * i), (fw_out, 2 * i)]))
        j = i // 2
        if i % 2 == 0:
            subs.append(("sgu", i, j, [(a_w_in, j), (a_w_out, j)]))
        else:
            subs.append(("pool", i, j, [(b_w_in, j), (bw_grp, j), (b_w_out, j)]))
        subs.append(("ffn", i, 1, [(fw_in, 2 * i + 1), (fw_out, 2 * i + 1)]))

    x2 = x.reshape(bsz * seq, d)
    fg = row(final_norm)
    weights = [w[l].astype(jnp.bfloat16) for w, l in subs[0][3]]
    for n, (kind, i, k, _) in enumerate(subs):
        casts = subs[n + 1][3] if n + 1 < len(subs) else []
        if kind == "ffn":
            x2, nxt = _ffn(x2, row(ffn_norm[i, k]), weights[0], weights[1], fg,
                           casts, final_norm=(n == len(subs) - 1))
        elif kind == "sgu":
            x2, nxt = _mixa(x2, row(mix_norm[i]), weights[0], row(a_v_norm[k]),
                            a_w_s[k], a_b_s[k].T, weights[1], casts)
        else:
            x3, nxt = _mixb(x2.reshape(bsz, seq, d), row(mix_norm[i]), weights[0],
                            weights[1], row(b_scale[k]), weights[2], casts)
            x2 = x3.reshape(bsz * seq, d)
        weights = list(nxt)
    return x2.reshape(bsz, seq, d)
```

```python
import functools

import jax
import jax.numpy as jnp
from jax import lax
from jax.experimental import pallas as pl
from jax.experimental.pallas import tpu as pltpu

D_MODEL = 1024
D_FF = 2816
CHUNK = 128
A_DIM = 2 * D_MODEL
A_HEADS = 8
A_HEAD_DIM = A_DIM // A_HEADS
POOL_WINDOWS = (2, 4, 8, 16)
B_GROUPS = len(POOL_WINDOWS)
B_GROUP_DIM = D_MODEL // B_GROUPS
EPS = 1e-6

HALO = 32
MXU_N = 256
LANES = 128
BF16_SUBLANES = 16
VMEM_LIMIT = 56 * 1024 * 1024

TM = 512


def _rmsnorm(x, g):
    ms = jnp.mean(x * x, axis=-1, keepdims=True)
    return x * lax.rsqrt(ms + EPS) * g


def _dot(a, b):
    return jnp.dot(a, b, preferred_element_type=jnp.float32)


def _resident(shape):
    return pl.BlockSpec(shape, lambda *_: (0,) * len(shape),
                        pipeline_mode=pl.Buffered(1))


def _cast_blocks(rows, n_steps):
    best = 1
    for nb in range(1, n_steps + 1):
        if rows % nb == 0 and (rows // nb) % BF16_SUBLANES == 0:
            best = nb
    return best


def _padded_cols(cols):
    return cols + LANES if (cols // LANES) % 8 == 0 else cols


def _cast_padded(w):
    pad = _padded_cols(w.shape[-1]) - w.shape[-1]
    return jnp.pad(w.astype(jnp.bfloat16), ((0, 0), (0, pad)))


def _cast_specs(casts, n_steps):
    in_specs, out_specs, out_shapes = [], [], []
    for w, layer in casts:
        _, rows, cols = w.shape
        nb = _cast_blocks(rows, n_steps)
        rb = rows // nb
        in_specs.append(pl.BlockSpec(
            (None, rb, cols),
            lambda i, layer=layer, nb=nb: (layer, jnp.minimum(i, nb - 1), 0)))
        out_specs.append(pl.BlockSpec(
            (rb, _padded_cols(cols)),
            lambda i, nb=nb: (jnp.minimum(i, nb - 1), 0)))
        out_shapes.append(
            jax.ShapeDtypeStruct((rows, _padded_cols(cols)), jnp.bfloat16))
    return in_specs, out_specs, out_shapes


def _run_casts(src_refs, dst_refs):
    for src, dst in zip(src_refs, dst_refs):
        rows, cols = src.shape
        dst[:, :cols] = src[...].astype(jnp.bfloat16)
        if dst.shape[1] > cols:
            dst[:, cols:] = jnp.zeros((rows, dst.shape[1] - cols), jnp.bfloat16)


def _sublayer_kernel(*refs, stage1, n_in, n_cast, n_tiles, n_parts, scale,
                     final_norm):
    assert n_parts[1] == 1 or not final_norm
    x_ref, xp_ref, wout_ref, fg_ref = refs[:4]
    ins = refs[4:4 + n_in]
    cast_src = refs[4 + n_in:4 + n_in + n_cast]
    o_ref = refs[4 + n_in + n_cast]
    cast_dst = refs[5 + n_in + n_cast:5 + n_in + 2 * n_cast]
    act_ref = refs[5 + n_in + 2 * n_cast]
    scratch = refs[6 + n_in + 2 * n_cast:]
    i = pl.program_id(0)

    row_parts, col_parts = n_parts

    def stage2_parts(slot):
        def part(r, c):
            height, width = x_ref.shape[0] // row_parts, D_MODEL // col_parts
            rows = slice(r * height, (r + 1) * height)
            cols = slice(c * width, (c + 1) * width)
            y = xp_ref[rows, cols] + scale * _dot(act_ref[slot, rows, :],
                                                  wout_ref[:, cols])
            if final_norm:
                y = _rmsnorm(y, fg_ref[...])
            o_ref[rows, cols] = y
        return [functools.partial(part, r, c)
                for c in range(col_parts) for r in range(row_parts)]

    no_parts = [lambda: None] * (row_parts * col_parts)

    @pl.when(i == 0)
    def _():
        _run_casts(cast_src, cast_dst)
        stage1(x_ref, ins, act_ref, scratch, 0, i, True, no_parts)

    @pl.when(jnp.logical_and(i > 0, i < n_tiles))
    def _():
        _run_casts(cast_src, cast_dst)
        slot = jnp.bitwise_and(i, 1)
        stage1(x_ref, ins, act_ref, scratch, slot, i, False,
               stage2_parts(1 - slot))

    @pl.when(i == n_tiles)
    def _():
        _run_casts(cast_src, cast_dst)
        for part in stage2_parts((n_tiles - 1) % 2):
            part()


def _sublayer_call(name, stage1, x2, w_out, fg, ins, in_specs, act_cols, scratch,
                   casts, *, n_parts, scale, final_norm):
    t, d = x2.shape
    n_tiles = t // TM
    c_in, c_out, c_shapes = _cast_specs(casts, n_tiles + 1)
    kern = functools.partial(
        _sublayer_kernel, stage1=stage1, n_in=len(ins), n_cast=len(casts),
        n_tiles=n_tiles, n_parts=n_parts, scale=scale, final_norm=final_norm)
    outs = pl.pallas_call(
        kern,
        grid=(n_tiles + 1,),
        in_specs=[
            pl.BlockSpec((TM, d), lambda i: (jnp.minimum(i, n_tiles - 1), 0)),
            pl.BlockSpec((TM, d), lambda i: (jnp.maximum(i - 1, 0), 0)),
            _resident(w_out.shape),
            _resident((1, d)),
        ] + in_specs + c_in,
        out_specs=[pl.BlockSpec((TM, d), lambda i: (jnp.maximum(i - 1, 0), 0))
                   ] + c_out,
        out_shape=[jax.ShapeDtypeStruct((t, d), jnp.float32)] + c_shapes,
        scratch_shapes=[pltpu.VMEM((2, TM, act_cols), jnp.bfloat16)] + scratch,
        compiler_params=pltpu.CompilerParams(
            dimension_semantics=("arbitrary",), vmem_limit_bytes=VMEM_LIMIT),
        name=name,
    )(x2, x2, w_out, fg, *ins, *[w for w, _ in casts])
    return outs[0], outs[1:]


def _ffn_stage1(x_ref, ins, act_ref, scratch, slot, step, first, out_parts):
    del step, first
    g_ref, win_ref = ins
    (h_ref,) = scratch
    out_parts[0]()
    h_ref[...] = _rmsnorm(x_ref[...], g_ref[...]).astype(jnp.bfloat16)
    for c in range(D_FF // MXU_N):
        lo = c * MXU_N
        gate = _dot(h_ref[...], win_ref[:, lo:lo + MXU_N])
        up = _dot(h_ref[...], win_ref[:, D_FF + lo:D_FF + lo + MXU_N])
        act_ref[slot, :, lo:lo + MXU_N] = (
            gate * jax.nn.sigmoid(gate) * up).astype(jnp.bfloat16)


def _ffn(x2, g, w_in, w_out, fg, casts, *, final_norm):
    d = x2.shape[1]
    return _sublayer_call(
        "ffn", _ffn_stage1, x2, w_out, fg, [g, w_in],
        [_resident((1, d)), _resident(w_in.shape)], D_FF,
        [pltpu.VMEM((TM, d), jnp.bfloat16)], casts,
        n_parts=(1, 1), scale=0.5, final_norm=final_norm)


def _gelu_tanh(x):
    k = -2.0 * 0.7978845608028654 * 1.4426950408889634
    return x / (1.0 + jnp.exp2(x * (k + (k * 0.044715) * (x * x))))


def _mixa_stage1(x_ref, ins, act_ref, scratch, slot, step, first, out_parts):
    del step, first
    g_ref, win_ref, vn_ref, ws_ref, bst_ref = ins
    h_ref, u_ref, v_ref = scratch
    tm = x_ref.shape[0]
    n_chunks = 4
    col_chunk = A_DIM // n_chunks
    out_parts = list(out_parts)
    n_after = (len(out_parts) - 1) // (n_chunks - 1)
    out_parts.pop(0)()
    h_ref[...] = _rmsnorm(x_ref[...], g_ref[...]).astype(jnp.bfloat16)
    ssq = jnp.zeros((tm, 1), jnp.float32)
    for c in range(n_chunks):
        lo = c * col_chunk
        v = _gelu_tanh(_dot(h_ref[...], win_ref[:, A_DIM + lo:A_DIM + lo + col_chunk]))
        for _ in range(min(n_after, len(out_parts))):
            out_parts.pop(0)()
        ssq = ssq + jnp.sum(v * v, axis=-1, keepdims=True)
        v_ref[:, lo:lo + col_chunk] = v
    inv = lax.rsqrt(ssq * (1.0 / A_DIM) + EPS)
    row = lax.broadcasted_iota(jnp.int32, (CHUNK, CHUNK), 0)
    col = lax.broadcasted_iota(jnp.int32, (CHUNK, CHUNK), 1)
    causal = col <= row

    def u_proj(hd):
        lo = hd * A_HEAD_DIM
        u_ref[:, lo:lo + A_HEAD_DIM] = _gelu_tanh(
            _dot(h_ref[...], win_ref[:, lo:lo + A_HEAD_DIM]))

    def gate_head(hd):
        lo = hd * A_HEAD_DIM
        ws = jnp.where(causal, ws_ref[hd], 0.0).astype(jnp.bfloat16)
        bias = bst_ref[:, hd:hd + 1]
        vn = (v_ref[:, lo:lo + A_HEAD_DIM] * inv
              * vn_ref[:, lo:lo + A_HEAD_DIM]).astype(jnp.bfloat16)
        for ck in range(tm // CHUNK):
            r0 = ck * CHUNK
            gate = _dot(ws, vn[r0:r0 + CHUNK, :]) + bias
            act_ref[slot, r0:r0 + CHUNK, lo:lo + A_HEAD_DIM] = (
                u_ref[r0:r0 + CHUNK, lo:lo + A_HEAD_DIM] * gate
            ).astype(jnp.bfloat16)

    u_proj(0)
    for hd in range(A_HEADS):
        if hd + 1 < A_HEADS:
            u_proj(hd + 1)
        gate_head(hd)
    assert not out_parts


def _mixa(x2, g, w_in, v_norm, w_s, b_st, w_out, fg, casts):
    d = x2.shape[1]
    return _sublayer_call(
        "mix_sgu", _mixa_stage1, x2, w_out, fg, [g, w_in, v_norm, w_s, b_st],
        [_resident((1, d)), _resident(w_in.shape), _resident((1, A_DIM)),
         _resident(w_s.shape), _resident(b_st.shape)], A_DIM,
        [pltpu.VMEM((TM, d), jnp.bfloat16),
         pltpu.VMEM((TM, A_DIM), jnp.float32),
         pltpu.VMEM((TM, A_DIM), jnp.float32)], casts,
        n_parts=(2, 4), scale=1.0, final_norm=False)


def _mixb_stage1(x_ref, ins, act_ref, scratch, slot, step, first, out_parts, *,
                 seq_tiles):
    g_ref, win_ref, wgrp_ref, scale_ref = ins
    ext_ref, sum_a_ref, sum_b_ref, h_ref = scratch
    tm = x_ref.shape[0]
    rows = HALO + tm
    s = lax.rem(step, seq_tiles)
    if first:
        ext_ref[0:HALO, :] = jnp.zeros((HALO, D_MODEL), jnp.float32)
    else:
        ext_ref[0:HALO, :] = jnp.where(s == 0, 0.0, ext_ref[tm:tm + HALO, :])
    pos = s * tm + lax.broadcasted_iota(jnp.int32, (tm, 1), 0)
    out_parts = list(out_parts)

    def in_proj(gi):
        cols = slice(gi * B_GROUP_DIM, (gi + 1) * B_GROUP_DIM)
        ext_ref[HALO:, cols] = _dot(h_ref[...], win_ref[:, cols])

    def pool_group(gi):
        w = POOL_WINDOWS[gi]
        lo = gi * B_GROUP_DIM
        cols = slice(lo, lo + B_GROUP_DIM)
        src, src_cols, level = ext_ref, cols, 0
        k = 1
        while 2 * k < w:
            level += 1
            r0 = 8 * level
            dst = sum_a_ref if level % 2 else sum_b_ref
            dst[r0:rows, :] = (src[r0:rows, src_cols]
                               + src[r0 - k:rows - k, src_cols])
            src, src_cols = dst, slice(None)
            k *= 2
        acc = src[HALO:rows, src_cols] + src[HALO - k:rows - k, src_cols]
        cnt = jnp.minimum(pos + 1, w).astype(jnp.float32)
        pooled = acc / cnt - ext_ref[HALO:, cols]
        y = _dot(pooled.astype(jnp.bfloat16),
                 wgrp_ref[lo:lo + B_GROUP_DIM, :B_GROUP_DIM])
        act_ref[slot, :, cols] = (y * scale_ref[:, cols]).astype(jnp.bfloat16)

    order = sorted(range(B_GROUPS), key=lambda gi: -POOL_WINDOWS[gi])
    out_parts.pop(0)()
    h_ref[...] = _rmsnorm(x_ref[...], g_ref[...]).astype(jnp.bfloat16)
    in_proj(order[0])
    for n, gi in enumerate(order):
        if n + 1 < len(order):
            in_proj(order[n + 1])
        if out_parts:
            out_parts.pop(0)()
        pool_group(gi)
    for out_part in out_parts:
        out_part()


def _mixb(x2, seq, g, w_in, w_grp, scale, w_out, fg, casts):
    d = x2.shape[1]
    stage1 = functools.partial(_mixb_stage1, seq_tiles=seq // TM)
    return _sublayer_call(
        "mix_pool", stage1, x2, w_out, fg, [g, w_in, w_grp, scale],
        [_resident((1, d)), _resident(w_in.shape), _resident(w_grp.shape),
         _resident((1, d))], d,
        [pltpu.VMEM((HALO + TM, d), jnp.float32),
         pltpu.VMEM((HALO + TM, B_GROUP_DIM), jnp.float32),
         pltpu.VMEM((HALO + TM, B_GROUP_DIM), jnp.float32),
         pltpu.VMEM((TM, d), jnp.bfloat16)], casts,
        n_parts=(1, 4), scale=1.0, final_norm=False)


def _stack3(w):
    return w.reshape((-1,) + w.shape[-2:])


def kernel(x, ffn_norm, ffn_w_in, ffn_w_out, mix_norm, a_w_in, a_v_norm, a_w_s,
           a_b_s, a_w_out, b_w_in, b_w_grp, b_scale, b_w_out, final_norm):
    bsz, seq, d = x.shape
    depth = ffn_norm.shape[0]
    row = lambda v: v.reshape(1, -1)
    fw_in, fw_out = _stack3(ffn_w_in), _stack3(ffn_w_out)
    bw_grp = b_w_grp.reshape(b_w_grp.shape[0], -1, b_w_grp.shape[-1])

    subs = []
    for i in range(depth):
        subs.append(("ffn", i, 0, [(fw_in, 2 * i), (fw_out, 2 * i)]))
        j = i // 2
        if i % 2 == 0:
            subs.append(("sgu", i, j, [(a_w_in, j), (a_w_out, j)]))
        else:
            subs.append(("pool", i, j, [(b_w_in, j), (bw_grp, j), (b_w_out, j)]))
        subs.append(("ffn", i, 1, [(fw_in, 2 * i + 1), (fw_out, 2 * i + 1)]))

    x2 = x.reshape(bsz * seq, d)
    fg = row(final_norm)
    weights = [_cast_padded(w[l]) for w, l in subs[0][3]]
    for n, (kind, i, k, _) in enumerate(subs):
        casts = subs[n + 1][3] if n + 1 < len(subs) else []
        if kind == "ffn":
            x2, nxt = _ffn(x2, row(ffn_norm[i, k]), weights[0], weights[1], fg,
                           casts, final_norm=(n == len(subs) - 1))
        elif kind == "sgu":
            x2, nxt = _mixa(x2, row(mix_norm[i]), weights[0], row(a_v_norm[k]),
                            a_w_s[k], a_b_s[k].T, weights[1], fg, casts)
        else:
            x2, nxt = _mixb(x2, seq, row(mix_norm[i]), weights[0], weights[1],
                            row(b_scale[k]), weights[2], fg, casts)
        weights = list(nxt)
    return x2.reshape(bsz, seq, d)
```

```python
import functools

import jax
import jax.numpy as jnp
from jax import lax
from jax.experimental import pallas as pl
from jax.experimental.pallas import tpu as pltpu

D_MODEL = 1024
D_FF = 2816
CHUNK = 128
A_DIM = 2 * D_MODEL
A_HEADS = 8
A_HEAD_DIM = A_DIM // A_HEADS
POOL_WINDOWS = (2, 4, 8, 16)
B_GROUPS = len(POOL_WINDOWS)
B_GROUP_DIM = D_MODEL // B_GROUPS
EPS = 1e-6

HALO = 32
MXU_N = 256
LANES = 128
BF16_SUBLANES = 16
VMEM_LIMIT = 56 * 1024 * 1024

TM = 512


def _rmsnorm(x, g):
    ms = jnp.mean(x * x, axis=-1, keepdims=True)
    return x * lax.rsqrt(ms + EPS) * g


def _dot(a, b):
    return jnp.dot(a, b, preferred_element_type=jnp.float32)


def _cast_blocks(rows, n_steps):
    best = 1
    for nb in range(1, n_steps + 1):
        if rows % nb == 0 and (rows // nb) % BF16_SUBLANES == 0:
            best = nb
    return best


def _padded_cols(cols):
    return cols + LANES if (cols // LANES) % 8 == 0 else cols


def _cast_padded(w):
    pad = _padded_cols(w.shape[-1]) - w.shape[-1]
    return jnp.pad(w.astype(jnp.bfloat16), ((0, 0), (0, pad)))


def _cast_specs(casts, n_steps):
    in_specs, out_specs, out_shapes = [], [], []
    for w, layer in casts:
        _, rows, cols = w.shape
        nb = _cast_blocks(rows, n_steps)
        rb = rows // nb
        in_specs.append(pl.BlockSpec(
            (None, rb, cols),
            lambda i, layer=layer, nb=nb: (layer, jnp.minimum(i, nb - 1), 0)))
        out_specs.append(pl.BlockSpec(
            (rb, _padded_cols(cols)),
            lambda i, nb=nb: (jnp.minimum(i, nb - 1), 0)))
        out_shapes.append(
            jax.ShapeDtypeStruct((rows, _padded_cols(cols)), jnp.bfloat16))
    return in_specs, out_specs, out_shapes


def _run_casts(src_refs, dst_refs):
    for src, dst in zip(src_refs, dst_refs):
        rows, cols = src.shape
        dst[:, :cols] = src[...].astype(jnp.bfloat16)
        if dst.shape[1] > cols:
            dst[:, cols:] = jnp.zeros((rows, dst.shape[1] - cols), jnp.bfloat16)


X_SLOTS = 3


def _sublayer_kernel(*refs, stage1, n_res, n_cast, n_tiles, n_parts, scale,
                     final_norm):
    assert n_parts[1] == 1 or not final_norm
    x_hbm = refs[0]
    res_hbm = refs[1:1 + n_res]
    cast_src = refs[1 + n_res:1 + n_res + n_cast]
    o_ref = refs[1 + n_res + n_cast]
    cast_dst = refs[2 + n_res + n_cast:2 + n_res + 2 * n_cast]
    rest = refs[2 + n_res + 2 * n_cast:]
    res = rest[:n_res]
    xbuf, x_sem, res_sem, act_ref = rest[n_res:n_res + 4]
    scratch = rest[n_res + 4:]
    wout_ref, fg_ref = res[:2]
    ins = res[2:]
    i = pl.program_id(0)
    tm = xbuf.shape[1]

    def x_copy(tile, slot):
        start = pl.multiple_of(tile * tm, tm)
        return pltpu.make_async_copy(x_hbm.at[pl.ds(start, tm), :],
                                     xbuf.at[slot], x_sem.at[slot])

    @pl.when(i == 0)
    def _():
        x_copy(0, 0).start()
        copies = [pltpu.make_async_copy(src, dst, res_sem.at[k])
                  for k, (src, dst) in enumerate(zip(res_hbm, res))]
        for copy in copies:
            copy.start()
        for copy in copies:
            copy.wait()

    @pl.when(i + 1 < n_tiles)
    def _():
        x_copy(i + 1, lax.rem(i + 1, X_SLOTS)).start()

    @pl.when(i < n_tiles)
    def _():
        x_copy(i, lax.rem(i, X_SLOTS)).wait()

    row_parts, col_parts = n_parts

    def stage2_parts(slot, xp_ref):
        def part(r, c):
            height, width = tm // row_parts, D_MODEL // col_parts
            rows = slice(r * height, (r + 1) * height)
            cols = slice(c * width, (c + 1) * width)
            y = xp_ref[rows, cols] + scale * _dot(act_ref[slot, rows, :],
                                                  wout_ref[:, cols])
            if final_norm:
                y = _rmsnorm(y, fg_ref[...])
            o_ref[rows, cols] = y
        return [functools.partial(part, r, c)
                for c in range(col_parts) for r in range(row_parts)]

    no_parts = [lambda: None] * (row_parts * col_parts)

    @pl.when(i == 0)
    def _():
        _run_casts(cast_src, cast_dst)
        stage1(xbuf.at[0], ins, act_ref, scratch, 0, i, True, no_parts)

    @pl.when(jnp.logical_and(i > 0, i < n_tiles))
    def _():
        _run_casts(cast_src, cast_dst)
        slot = jnp.bitwise_and(i, 1)
        x_ref = xbuf.at[lax.rem(i, X_SLOTS)]
        xp_ref = xbuf.at[lax.rem(i + X_SLOTS - 1, X_SLOTS)]
        stage1(x_ref, ins, act_ref, scratch, slot, i, False,
               stage2_parts(1 - slot, xp_ref))

    @pl.when(i == n_tiles)
    def _():
        _run_casts(cast_src, cast_dst)
        xp_ref = xbuf.at[(n_tiles - 1) % X_SLOTS]
        for part in stage2_parts((n_tiles - 1) % 2, xp_ref):
            part()


def _sublayer_call(name, stage1, x2, w_out, fg, ins, act_cols, scratch, casts, *,
                   n_parts, scale, final_norm):
    t, d = x2.shape
    n_tiles = t // TM
    resident = [w_out, fg] + ins
    c_in, c_out, c_shapes = _cast_specs(casts, n_tiles + 1)
    kern = functools.partial(
        _sublayer_kernel, stage1=stage1, n_res=len(resident), n_cast=len(casts),
        n_tiles=n_tiles, n_parts=n_parts, scale=scale, final_norm=final_norm)
    outs = pl.pallas_call(
        kern,
        grid=(n_tiles + 1,),
        in_specs=[pl.BlockSpec(memory_space=pl.ANY)] * (1 + len(resident)) + c_in,
        out_specs=[pl.BlockSpec((TM, d), lambda i: (jnp.maximum(i - 1, 0), 0))
                   ] + c_out,
        out_shape=[jax.ShapeDtypeStruct((t, d), jnp.float32)] + c_shapes,
        scratch_shapes=[pltpu.VMEM(w.shape, w.dtype) for w in resident] + [
            pltpu.VMEM((X_SLOTS, TM, d), jnp.float32),
            pltpu.SemaphoreType.DMA((X_SLOTS,)),
            pltpu.SemaphoreType.DMA((len(resident),)),
            pltpu.VMEM((2, TM, act_cols), jnp.bfloat16)] + scratch,
        compiler_params=pltpu.CompilerParams(
            dimension_semantics=("arbitrary",), vmem_limit_bytes=VMEM_LIMIT),
        name=name,
    )(x2, *resident, *[w for w, _ in casts])
    return outs[0], outs[1:]


def _ffn_stage1(x_ref, ins, act_ref, scratch, slot, step, first, out_parts):
    del step, first
    g_ref, win_ref = ins
    (h_ref,) = scratch
    out_parts[0]()
    h_ref[...] = _rmsnorm(x_ref[...], g_ref[...]).astype(jnp.bfloat16)
    for c in range(D_FF // MXU_N):
        lo = c * MXU_N
        gate = _dot(h_ref[...], win_ref[:, lo:lo + MXU_N])
        up = _dot(h_ref[...], win_ref[:, D_FF + lo:D_FF + lo + MXU_N])
        act_ref[slot, :, lo:lo + MXU_N] = (
            gate * jax.nn.sigmoid(gate) * up).astype(jnp.bfloat16)


def _ffn(x2, g, w_in, w_out, fg, casts, *, final_norm):
    d = x2.shape[1]
    return _sublayer_call(
        "ffn", _ffn_stage1, x2, w_out, fg, [g, w_in], D_FF,
        [pltpu.VMEM((TM, d), jnp.bfloat16)], casts,
        n_parts=(1, 1), scale=0.5, final_norm=final_norm)


def _gelu_tanh(x):
    k = -2.0 * 0.7978845608028654 * 1.4426950408889634
    return x / (1.0 + jnp.exp2(x * (k + (k * 0.044715) * (x * x))))


def _mixa_stage1(x_ref, ins, act_ref, scratch, slot, step, first, out_parts):
    del step, first
    g_ref, win_ref, vn_ref, ws_ref, bst_ref = ins
    h_ref, u_ref, v_ref = scratch
    tm = x_ref.shape[0]
    n_chunks = 4
    col_chunk = A_DIM // n_chunks
    out_parts = list(out_parts)
    n_after = (len(out_parts) - 1) // (n_chunks - 1)
    out_parts.pop(0)()
    h_ref[...] = _rmsnorm(x_ref[...], g_ref[...]).astype(jnp.bfloat16)
    ssq = jnp.zeros((tm, 1), jnp.float32)
    for c in range(n_chunks):
        lo = c * col_chunk
        v = _gelu_tanh(_dot(h_ref[...], win_ref[:, A_DIM + lo:A_DIM + lo + col_chunk]))
        for _ in range(min(n_after, len(out_parts))):
            out_parts.pop(0)()
        ssq = ssq + jnp.sum(v * v, axis=-1, keepdims=True)
        v_ref[:, lo:lo + col_chunk] = v
    inv = lax.rsqrt(ssq * (1.0 / A_DIM) + EPS)
    row = lax.broadcasted_iota(jnp.int32, (CHUNK, CHUNK), 0)
    col = lax.broadcasted_iota(jnp.int32, (CHUNK, CHUNK), 1)
    causal = col <= row

    def u_proj(hd):
        lo = hd * A_HEAD_DIM
        u_ref[:, lo:lo + A_HEAD_DIM] = _gelu_tanh(
            _dot(h_ref[...], win_ref[:, lo:lo + A_HEAD_DIM]))

    def gate_head(hd):
        lo = hd * A_HEAD_DIM
        ws = jnp.where(causal, ws_ref[hd], 0.0).astype(jnp.bfloat16)
        bias = bst_ref[:, hd:hd + 1]
        vn = (v_ref[:, lo:lo + A_HEAD_DIM] * inv
              * vn_ref[:, lo:lo + A_HEAD_DIM]).astype(jnp.bfloat16)
        for ck in range(tm // CHUNK):
            r0 = ck * CHUNK
            gate = _dot(ws, vn[r0:r0 + CHUNK, :]) + bias
            act_ref[slot, r0:r0 + CHUNK, lo:lo + A_HEAD_DIM] = (
                u_ref[r0:r0 + CHUNK, lo:lo + A_HEAD_DIM] * gate
            ).astype(jnp.bfloat16)

    u_proj(0)
    for hd in range(A_HEADS):
        if hd + 1 < A_HEADS:
            u_proj(hd + 1)
        gate_head(hd)
    assert not out_parts


def _mixa(x2, g, w_in, v_norm, w_s, b_st, w_out, fg, casts):
    d = x2.shape[1]
    return _sublayer_call(
        "mix_sgu", _mixa_stage1, x2, w_out, fg, [g, w_in, v_norm, w_s, b_st], A_DIM,
        [pltpu.VMEM((TM, d), jnp.bfloat16),
         pltpu.VMEM((TM, A_DIM), jnp.float32),
         pltpu.VMEM((TM, A_DIM), jnp.float32)], casts,
        n_parts=(2, 4), scale=1.0, final_norm=False)


def _mixb_stage1(x_ref, ins, act_ref, scratch, slot, step, first, out_parts, *,
                 seq_tiles):
    g_ref, win_ref, wgrp_ref, scale_ref = ins
    ext_ref, sum_a_ref, sum_b_ref, h_ref = scratch
    tm = x_ref.shape[0]
    rows = HALO + tm
    s = lax.rem(step, seq_tiles)
    if first:
        ext_ref[0:HALO, :] = jnp.zeros((HALO, D_MODEL), jnp.float32)
    else:
        ext_ref[0:HALO, :] = jnp.where(s == 0, 0.0, ext_ref[tm:tm + HALO, :])
    pos = s * tm + lax.broadcasted_iota(jnp.int32, (tm, 1), 0)
    out_parts = list(out_parts)

    def in_proj(gi):
        cols = slice(gi * B_GROUP_DIM, (gi + 1) * B_GROUP_DIM)
        ext_ref[HALO:, cols] = _dot(h_ref[...], win_ref[:, cols])

    def pool_group(gi):
        w = POOL_WINDOWS[gi]
        lo = gi * B_GROUP_DIM
        cols = slice(lo, lo + B_GROUP_DIM)
        src, src_cols, level = ext_ref, cols, 0
        k = 1
        while 2 * k < w:
            level += 1
            r0 = 8 * level
            dst = sum_a_ref if level % 2 else sum_b_ref
            dst[r0:rows, :] = (src[r0:rows, src_cols]
                               + src[r0 - k:rows - k, src_cols])
            src, src_cols = dst, slice(None)
            k *= 2
        acc = src[HALO:rows, src_cols] + src[HALO - k:rows - k, src_cols]
        cnt = jnp.minimum(pos + 1, w).astype(jnp.float32)
        pooled = acc / cnt - ext_ref[HALO:, cols]
        y = _dot(pooled.astype(jnp.bfloat16),
                 wgrp_ref[lo:lo + B_GROUP_DIM, :B_GROUP_DIM])
        act_ref[slot, :, cols] = (y * scale_ref[:, cols]).astype(jnp.bfloat16)

    order = sorted(range(B_GROUPS), key=lambda gi: -POOL_WINDOWS[gi])
    out_parts.pop(0)()
    h_ref[...] = _rmsnorm(x_ref[...], g_ref[...]).astype(jnp.bfloat16)
    in_proj(order[0])
    for n, gi in enumerate(order):
        if n + 1 < len(order):
            in_proj(order[n + 1])
        if out_parts:
            out_parts.pop(0)()
        pool_group(gi)
    for out_part in out_parts:
        out_part()


def _mixb(x2, seq, g, w_in, w_grp, scale, w_out, fg, casts):
    d = x2.shape[1]
    stage1 = functools.partial(_mixb_stage1, seq_tiles=seq // TM)
    return _sublayer_call(
        "mix_pool", stage1, x2, w_out, fg, [g, w_in, w_grp, scale], d,
        [pltpu.VMEM((HALO + TM, d), jnp.float32),
         pltpu.VMEM((HALO + TM, B_GROUP_DIM), jnp.float32),
         pltpu.VMEM((HALO + TM, B_GROUP_DIM), jnp.float32),
         pltpu.VMEM((TM, d), jnp.bfloat16)], casts,
        n_parts=(1, 4), scale=1.0, final_norm=False)


def _stack3(w):
    return w.reshape((-1,) + w.shape[-2:])


def kernel(x, ffn_norm, ffn_w_in, ffn_w_out, mix_norm, a_w_in, a_v_norm, a_w_s,
           a_b_s, a_w_out, b_w_in, b_w_grp, b_scale, b_w_out, final_norm):
    bsz, seq, d = x.shape
    depth = ffn_norm.shape[0]
    row = lambda v: v.reshape(1, -1)
    fw_in, fw_out = _stack3(ffn_w_in), _stack3(ffn_w_out)
    bw_grp = b_w_grp.reshape(b_w_grp.shape[0], -1, b_w_grp.shape[-1])

    subs = []
    for i in range(depth):
        subs.append(("ffn", i, 0, [(fw_in, 2 * i), (fw_out, 2 * i)]))
        j = i // 2
        if i % 2 == 0:
            subs.append(("sgu", i, j, [(a_w_in, j), (a_w_out, j)]))
        else:
            subs.append(("pool", i, j, [(b_w_in, j), (bw_grp, j), (b_w_out, j)]))
        subs.append(("ffn", i, 1, [(fw_in, 2 * i + 1), (fw_out, 2 * i + 1)]))

    x2 = x.reshape(bsz * seq, d)
    fg = row(final_norm)
    weights = [_cast_padded(w[l]) for w, l in subs[0][3]]
    for n, (kind, i, k, _) in enumerate(subs):
        casts = subs[n + 1][3] if n + 1 < len(subs) else []
        if kind == "ffn":
            x2, nxt = _ffn(x2, row(ffn_norm[i, k]), weights[0], weights[1], fg,
                           casts, final_norm=(n == len(subs) - 1))
        elif kind == "sgu":
            x2, nxt = _mixa(x2, row(mix_norm[i]), weights[0], row(a_v_norm[k]),
                            a_w_s[k], a_b_s[k].T, weights[1], fg, casts)
        else:
            x2, nxt = _mixb(x2, seq, row(mix_norm[i]), weights[0], weights[1],
                            row(b_scale[k]), weights[2], fg, casts)
        weights = list(nxt)
    return x2.reshape(bsz, seq, d)
```

```python
import functools

import jax
import jax.numpy as jnp
from jax import lax
from jax.experimental import pallas as pl
from jax.experimental.pallas import tpu as pltpu

D_MODEL = 1024
D_FF = 2816
CHUNK = 128
A_DIM = 2 * D_MODEL
A_HEADS = 8
A_HEAD_DIM = A_DIM // A_HEADS
POOL_WINDOWS = (2, 4, 8, 16)
B_GROUPS = len(POOL_WINDOWS)
B_GROUP_DIM = D_MODEL // B_GROUPS
EPS = 1e-6

HALO = 32
MXU_N = 256
LANES = 128
BF16_SUBLANES = 16
VMEM_LIMIT = 56 * 1024 * 1024

TM = 512


def _rmsnorm(x, g):
    ms = jnp.mean(x * x, axis=-1, keepdims=True)
    return x * lax.rsqrt(ms + EPS) * g


def _dot(a, b):
    return jnp.dot(a, b, preferred_element_type=jnp.float32)


def _resident(shape):
    return pl.BlockSpec(shape, lambda *_: (0,) * len(shape),
                        pipeline_mode=pl.Buffered(1))


def _cast_blocks(rows, n_steps):
    best = 1
    for nb in range(1, n_steps + 1):
        if rows % nb == 0 and (rows // nb) % BF16_SUBLANES == 0:
            best = nb
    return best


def _padded_cols(cols):
    return cols + LANES if (cols // LANES) % 8 == 0 else cols


def _cast_padded(w):
    pad = _padded_cols(w.shape[-1]) - w.shape[-1]
    return jnp.pad(w.astype(jnp.bfloat16), ((0, 0), (0, pad)))


def _cast_specs(casts, n_steps):
    in_specs, out_specs, out_shapes = [], [], []
    for w, layer in casts:
        _, rows, cols = w.shape
        nb = _cast_blocks(rows, n_steps)
        rb = rows // nb
        in_specs.append(pl.BlockSpec(
            (None, rb, cols),
            lambda i, layer=layer, nb=nb: (layer, jnp.minimum(i, nb - 1), 0)))
        out_specs.append(pl.BlockSpec(
            (rb, _padded_cols(cols)),
            lambda i, nb=nb: (jnp.minimum(i, nb - 1), 0)))
        out_shapes.append(
            jax.ShapeDtypeStruct((rows, _padded_cols(cols)), jnp.bfloat16))
    return in_specs, out_specs, out_shapes


def _run_casts(src_refs, dst_refs):
    for src, dst in zip(src_refs, dst_refs):
        rows, cols = src.shape
        dst[:, :cols] = src[...].astype(jnp.bfloat16)
        if dst.shape[1] > cols:
            dst[:, cols:] = jnp.zeros((rows, dst.shape[1] - cols), jnp.bfloat16)


def _sublayer_kernel(*refs, stage1, n_in, n_cast, n_tiles, n_parts, scale,
                     final_norm):
    assert n_parts[1] == 1 or not final_norm
    x_ref, xp_ref, wout_ref, fg_ref = refs[:4]
    ins = refs[4:4 + n_in]
    cast_src = refs[4 + n_in:4 + n_in + n_cast]
    o_ref = refs[4 + n_in + n_cast]
    cast_dst = refs[5 + n_in + n_cast:5 + n_in + 2 * n_cast]
    act_ref = refs[5 + n_in + 2 * n_cast]
    scratch = refs[6 + n_in + 2 * n_cast:]
    i = pl.program_id(0)

    row_parts, col_parts = n_parts

    def stage2_parts(slot):
        def part(r, c):
            height, width = x_ref.shape[0] // row_parts, D_MODEL // col_parts
            rows = slice(r * height, (r + 1) * height)
            cols = slice(c * width, (c + 1) * width)
            y = xp_ref[rows, cols] + scale * _dot(act_ref[slot, rows, :],
                                                  wout_ref[:, cols])
            if final_norm:
                y = _rmsnorm(y, fg_ref[...])
            o_ref[rows, cols] = y
        return [functools.partial(part, r, c)
                for c in range(col_parts) for r in range(row_parts)]

    no_parts = [lambda: None] * (row_parts * col_parts)

    @pl.when(i == 0)
    def _():
        _run_casts(cast_src, cast_dst)
        stage1(x_ref, ins, act_ref, scratch, 0, i, True, no_parts)

    @pl.when(jnp.logical_and(i > 0, i < n_tiles))
    def _():
        _run_casts(cast_src, cast_dst)
        slot = jnp.bitwise_and(i, 1)
        stage1(x_ref, ins, act_ref, scratch, slot, i, False,
               stage2_parts(1 - slot))

    @pl.when(i == n_tiles)
    def _():
        _run_casts(cast_src, cast_dst)
        for part in stage2_parts((n_tiles - 1) % 2):
            part()


def _sublayer_call(name, stage1, x2, w_out, fg, ins, in_specs, act_cols, scratch,
                   casts, *, n_parts, scale, final_norm):
    t, d = x2.shape
    n_tiles = t // TM
    c_in, c_out, c_shapes = _cast_specs(casts, n_tiles + 1)
    kern = functools.partial(
        _sublayer_kernel, stage1=stage1, n_in=len(ins), n_cast=len(casts),
        n_tiles=n_tiles, n_parts=n_parts, scale=scale, final_norm=final_norm)
    outs = pl.pallas_call(
        kern,
        grid=(n_tiles + 1,),
        in_specs=[
            pl.BlockSpec((TM, d), lambda i: (jnp.minimum(i, n_tiles - 1), 0)),
            pl.BlockSpec((TM, d), lambda i: (jnp.maximum(i - 1, 0), 0)),
            _resident(w_out.shape),
            _resident((1, d)),
        ] + in_specs + c_in,
        out_specs=[pl.BlockSpec((TM, d), lambda i: (jnp.maximum(i - 1, 0), 0))
                   ] + c_out,
        out_shape=[jax.ShapeDtypeStruct((t, d), jnp.float32)] + c_shapes,
        scratch_shapes=[pltpu.VMEM((2, TM, act_cols), jnp.bfloat16)] + scratch,
        compiler_params=pltpu.CompilerParams(
            dimension_semantics=("arbitrary",), vmem_limit_bytes=VMEM_LIMIT),
        name=name,
    )(x2, x2, w_out, fg, *ins, *[w for w, _ in casts])
    return outs[0], outs[1:]


FFN_SUBTILES = 2


def _ffn_kernel(*refs, n_cast, final_norm):
    x_ref, g_ref, win_ref, wout_ref, fg_ref = refs[:5]
    cast_src = refs[5:5 + n_cast]
    o_ref = refs[5 + n_cast]
    cast_dst = refs[6 + n_cast:6 + 2 * n_cast]
    xg_ref, a_ref, r_ref = refs[6 + 2 * n_cast:]
    _run_casts(cast_src, cast_dst)
    for r0 in range(0, x_ref.shape[0], TM):
        rows = pl.ds(r0, TM)
        x = x_ref[rows, :]
        xg_ref[rows, :] = (x * g_ref[...]).astype(jnp.bfloat16)
        r = lax.rsqrt(jnp.mean(x * x, axis=-1, keepdims=True) + EPS)
        r_ref[rows, :] = jnp.broadcast_to(r, (TM, MXU_N))
        for c in range(D_FF // MXU_N):
            lo = c * MXU_N
            gate = _dot(xg_ref[rows, :], win_ref[:, lo:lo + MXU_N]) * r_ref[rows, :]
            up = _dot(xg_ref[rows, :],
                      win_ref[:, D_FF + lo:D_FF + lo + MXU_N]) * r_ref[rows, :]
            a_ref[rows, lo:lo + MXU_N] = (
                gate * jax.nn.sigmoid(gate) * up).astype(jnp.bfloat16)
        y = x_ref[rows, :] + 0.5 * _dot(a_ref[rows, :], wout_ref[:, :D_MODEL])
        if final_norm:
            y = _rmsnorm(y, fg_ref[...])
        o_ref[rows, :] = y


def _ffn(x2, g, w_in, w_out, fg, casts, *, final_norm):
    t, d = x2.shape
    tb = FFN_SUBTILES * TM
    n_steps = t // tb
    c_in, c_out, c_shapes = _cast_specs(casts, n_steps)
    kern = functools.partial(_ffn_kernel, n_cast=len(casts), final_norm=final_norm)
    outs = pl.pallas_call(
        kern,
        grid=(n_steps,),
        in_specs=[
            pl.BlockSpec((tb, d), lambda i: (i, 0)),
            _resident((1, d)),
            _resident(w_in.shape),
            _resident(w_out.shape),
            _resident((1, d)),
        ] + c_in,
        out_specs=[pl.BlockSpec((tb, d), lambda i: (i, 0))] + c_out,
        out_shape=[jax.ShapeDtypeStruct((t, d), jnp.float32)] + c_shapes,
        scratch_shapes=[
            pltpu.VMEM((tb, d), jnp.bfloat16),
            pltpu.VMEM((tb, D_FF), jnp.bfloat16),
            pltpu.VMEM((tb, MXU_N), jnp.float32),
        ],
        compiler_params=pltpu.CompilerParams(
            dimension_semantics=("arbitrary",), vmem_limit_bytes=VMEM_LIMIT),
        name="ffn",
    )(x2, g, w_in, w_out, fg, *[w for w, _ in casts])
    return outs[0], outs[1:]


def _gelu_tanh(x):
    k = -2.0 * 0.7978845608028654 * 1.4426950408889634
    return x / (1.0 + jnp.exp2(x * (k + (k * 0.044715) * (x * x))))


def _mixa_stage1(x_ref, ins, act_ref, scratch, slot, step, first, out_parts):
    del step, first
    g_ref, win_ref, vn_ref, ws_ref, bst_ref = ins
    h_ref, u_ref, v_ref = scratch
    tm = x_ref.shape[0]
    n_chunks = 4
    col_chunk = A_DIM // n_chunks
    out_parts = list(out_parts)
    n_after = (len(out_parts) - 1) // (n_chunks - 1)
    out_parts.pop(0)()
    h_ref[...] = _rmsnorm(x_ref[...], g_ref[...]).astype(jnp.bfloat16)
    ssq = jnp.zeros((tm, 1), jnp.float32)
    for c in range(n_chunks):
        lo = c * col_chunk
        v = _gelu_tanh(_dot(h_ref[...], win_ref[:, A_DIM + lo:A_DIM + lo + col_chunk]))
        for _ in range(min(n_after, len(out_parts))):
            out_parts.pop(0)()
        ssq = ssq + jnp.sum(v * v, axis=-1, keepdims=True)
        v_ref[:, lo:lo + col_chunk] = v
    inv = lax.rsqrt(ssq * (1.0 / A_DIM) + EPS)
    row = lax.broadcasted_iota(jnp.int32, (CHUNK, CHUNK), 0)
    col = lax.broadcasted_iota(jnp.int32, (CHUNK, CHUNK), 1)
    causal = col <= row

    def u_proj(hd):
        lo = hd * A_HEAD_DIM
        u_ref[:, lo:lo + A_HEAD_DIM] = _gelu_tanh(
            _dot(h_ref[...], win_ref[:, lo:lo + A_HEAD_DIM]))

    def gate_head(hd):
        lo = hd * A_HEAD_DIM
        ws = jnp.where(causal, ws_ref[hd], 0.0).astype(jnp.bfloat16)
        bias = bst_ref[:, hd:hd + 1]
        vn = (v_ref[:, lo:lo + A_HEAD_DIM] * inv
              * vn_ref[:, lo:lo + A_HEAD_DIM]).astype(jnp.bfloat16)
        for ck in range(tm // CHUNK):
            r0 = ck * CHUNK
            gate = _dot(ws, vn[r0:r0 + CHUNK, :]) + bias
            act_ref[slot, r0:r0 + CHUNK, lo:lo + A_HEAD_DIM] = (
                u_ref[r0:r0 + CHUNK, lo:lo + A_HEAD_DIM] * gate
            ).astype(jnp.bfloat16)

    u_proj(0)
    for hd in range(A_HEADS):
        if hd + 1 < A_HEADS:
            u_proj(hd + 1)
        gate_head(hd)
    assert not out_parts


def _mixa(x2, g, w_in, v_norm, w_s, b_st, w_out, fg, casts):
    d = x2.shape[1]
    return _sublayer_call(
        "mix_sgu", _mixa_stage1, x2, w_out, fg, [g, w_in, v_norm, w_s, b_st],
        [_resident((1, d)), _resident(w_in.shape), _resident((1, A_DIM)),
         _resident(w_s.shape), _resident(b_st.shape)], A_DIM,
        [pltpu.VMEM((TM, d), jnp.bfloat16),
         pltpu.VMEM((TM, A_DIM), jnp.float32),
         pltpu.VMEM((TM, A_DIM), jnp.float32)], casts,
        n_parts=(2, 4), scale=1.0, final_norm=False)


def _mixb_stage1(x_ref, ins, act_ref, scratch, slot, step, first, out_parts, *,
                 seq_tiles):
    g_ref, win_ref, wgrp_ref, scale_ref = ins
    ext_ref, sum_a_ref, sum_b_ref, h_ref = scratch
    tm = x_ref.shape[0]
    rows = HALO + tm
    s = lax.rem(step, seq_tiles)
    if first:
        ext_ref[0:HALO, :] = jnp.zeros((HALO, D_MODEL), jnp.float32)
    else:
        ext_ref[0:HALO, :] = jnp.where(s == 0, 0.0, ext_ref[tm:tm + HALO, :])
    pos = s * tm + lax.broadcasted_iota(jnp.int32, (tm, 1), 0)
    out_parts = list(out_parts)

    def in_proj(gi):
        cols = slice(gi * B_GROUP_DIM, (gi + 1) * B_GROUP_DIM)
        ext_ref[HALO:, cols] = _dot(h_ref[...], win_ref[:, cols])

    def pool_group(gi):
        w = POOL_WINDOWS[gi]
        lo = gi * B_GROUP_DIM
        cols = slice(lo, lo + B_GROUP_DIM)
        src, src_cols, level = ext_ref, cols, 0
        k = 1
        while 2 * k < w:
            level += 1
            r0 = 8 * level
            dst = sum_a_ref if level % 2 else sum_b_ref
            dst[r0:rows, :] = (src[r0:rows, src_cols]
                               + src[r0 - k:rows - k, src_cols])
            src, src_cols = dst, slice(None)
            k *= 2
        acc = src[HALO:rows, src_cols] + src[HALO - k:rows - k, src_cols]
        cnt = jnp.minimum(pos + 1, w).astype(jnp.float32)
        pooled = acc / cnt - ext_ref[HALO:, cols]
        y = _dot(pooled.astype(jnp.bfloat16),
                 wgrp_ref[lo:lo + B_GROUP_DIM, :B_GROUP_DIM])
        act_ref[slot, :, cols] = (y * scale_ref[:, cols]).astype(jnp.bfloat16)

    order = sorted(range(B_GROUPS), key=lambda gi: -POOL_WINDOWS[gi])
    out_parts.pop(0)()
    h_ref[...] = _rmsnorm(x_ref[...], g_ref[...]).astype(jnp.bfloat16)
    in_proj(order[0])
    for n, gi in enumerate(order):
        if n + 1 < len(order):
            in_proj(order[n + 1])
        if out_parts:
            out_parts.pop(0)()
        pool_group(gi)
    for out_part in out_parts:
        out_part()


def _mixb(x2, seq, g, w_in, w_grp, scale, w_out, fg, casts):
    d = x2.shape[1]
    stage1 = functools.partial(_mixb_stage1, seq_tiles=seq // TM)
    return _sublayer_call(
        "mix_pool", stage1, x2, w_out, fg, [g, w_in, w_grp, scale],
        [_resident((1, d)), _resident(w_in.shape), _resident(w_grp.shape),
         _resident((1, d))], d,
        [pltpu.VMEM((HALO + TM, d), jnp.float32),
         pltpu.VMEM((HALO + TM, B_GROUP_DIM), jnp.float32),
         pltpu.VMEM((HALO + TM, B_GROUP_DIM), jnp.float32),
         pltpu.VMEM((TM, d), jnp.bfloat16)], casts,
        n_parts=(1, 4), scale=1.0, final_norm=False)


def _stack3(w):
    return w.reshape((-1,) + w.shape[-2:])


def kernel(x, ffn_norm, ffn_w_in, ffn_w_out, mix_norm, a_w_in, a_v_norm, a_w_s,
           a_b_s, a_w_out, b_w_in, b_w_grp, b_scale, b_w_out, final_norm):
    bsz, seq, d = x.shape
    depth = ffn_norm.shape[0]
    row = lambda v: v.reshape(1, -1)
    fw_in, fw_out = _stack3(ffn_w_in), _stack3(ffn_w_out)
    bw_grp = b_w_grp.reshape(b_w_grp.shape[0], -1, b_w_grp.shape[-1])

    subs = []
    for i in range(depth):
        subs.append(("ffn", i, 0, [(fw_in, 2 * i), (fw_out, 2 * i)]))
        j = i // 2
        if i % 2 == 0:
            subs.append(("sgu", i, j, [(a_w_in, j), (a_w_out, j)]))
        else:
            subs.append(("pool", i, j, [(b_w_in, j), (bw_grp, j), (b_w_out, j)]))
        subs.append(("ffn", i, 1, [(fw_in, 2 * i + 1), (fw_out, 2 * i + 1)]))

    x2 = x.reshape(bsz * seq, d)
    fg = row(final_norm)
    weights = [_cast_padded(w[l]) for w, l in subs[0][3]]
    for n, (kind, i, k, _) in enumerate(subs):
        casts = subs[n + 1][3] if n + 1 < len(subs) else []
        if kind == "ffn":
            x2, nxt = _ffn(x2, row(ffn_norm[i, k]), weights[0], weights[1], fg,
                           casts, final_norm=(n == len(subs) - 1))
        elif kind == "sgu":
            x2, nxt = _mixa(x2, row(mix_norm[i]), weights[0], row(a_v_norm[k]),
                            a_w_s[k], a_b_s[k].T, weights[1], fg, casts)
        else:
            x2, nxt = _mixb(x2, seq, row(mix_norm[i]), weights[0], weights[1],
                            row(b_scale[k]), weights[2], fg, casts)
        weights = list(nxt)
    return x2.reshape(bsz, seq, d)
```

```python
import functools

import jax
import jax.numpy as jnp
from jax import lax
from jax.experimental import pallas as pl
from jax.experimental.pallas import tpu as pltpu

D_MODEL = 1024
D_FF = 2816
CHUNK = 128
A_DIM = 2 * D_MODEL
A_HEADS = 8
A_HEAD_DIM = A_DIM // A_HEADS
POOL_WINDOWS = (2, 4, 8, 16)
B_GROUPS = len(POOL_WINDOWS)
B_GROUP_DIM = D_MODEL // B_GROUPS
EPS = 1e-6

HALO = 32
MXU_N = 256
LANES = 128
BF16_SUBLANES = 16
VMEM_LIMIT = 56 * 1024 * 1024

TM = 512
POOL_TM = 1024


def _rmsnorm(x, g):
    ms = jnp.mean(x * x, axis=-1, keepdims=True)
    return x * lax.rsqrt(ms + EPS) * g


def _dot(a, b):
    return jnp.dot(a, b, preferred_element_type=jnp.float32)


def _resident(shape):
    return pl.BlockSpec(shape, lambda *_: (0,) * len(shape),
                        pipeline_mode=pl.Buffered(1))


def _cast_blocks(rows, n_steps):
    best = 1
    for nb in range(1, n_steps + 1):
        if rows % nb == 0 and (rows // nb) % BF16_SUBLANES == 0:
            best = nb
    return best


def _padded_cols(cols):
    return cols + LANES if (cols // LANES) % 8 == 0 else cols


def _cast_padded(w):
    pad = _padded_cols(w.shape[-1]) - w.shape[-1]
    wb = w.astype(jnp.bfloat16)
    if pad == 0:
        return wb
    return jnp.concatenate([wb, jnp.zeros((w.shape[0], pad), jnp.bfloat16)], axis=1)


def _cast_specs(casts, n_steps):
    in_specs, out_specs, out_shapes = [], [], []
    for w, layer in casts:
        _, rows, cols = w.shape
        nb = _cast_blocks(rows, n_steps)
        rb = rows // nb
        in_specs.append(pl.BlockSpec(
            (None, rb, cols),
            lambda i, layer=layer, nb=nb: (layer, jnp.minimum(i, nb - 1), 0)))
        out_specs.append(pl.BlockSpec(
            (rb, _padded_cols(cols)),
            lambda i, nb=nb: (jnp.minimum(i, nb - 1), 0)))
        out_shapes.append(
            jax.ShapeDtypeStruct((rows, _padded_cols(cols)), jnp.bfloat16))
    return in_specs, out_specs, out_shapes


def _run_casts(src_refs, dst_refs):
    for src, dst in zip(src_refs, dst_refs):
        rows, cols = src.shape
        dst[:, :cols] = src[...].astype(jnp.bfloat16)
        if dst.shape[1] > cols:
            dst[:, cols:] = jnp.zeros((rows, dst.shape[1] - cols), jnp.bfloat16)


def _sublayer_kernel(*refs, stage1, n_in, n_cast, n_tiles, n_parts, scale,
                     final_norm):
    assert n_parts[1] == 1 or not final_norm
    x_ref, xp_ref, wout_ref, fg_ref = refs[:4]
    ins = refs[4:4 + n_in]
    cast_src = refs[4 + n_in:4 + n_in + n_cast]
    o_ref = refs[4 + n_in + n_cast]
    cast_dst = refs[5 + n_in + n_cast:5 + n_in + 2 * n_cast]
    act_ref = refs[5 + n_in + 2 * n_cast]
    scratch = refs[6 + n_in + 2 * n_cast:]
    i = pl.program_id(0)

    row_parts, col_parts = n_parts

    def stage2_parts(slot):
        def part(r, c):
            height, width = x_ref.shape[0] // row_parts, D_MODEL // col_parts
            rows = slice(r * height, (r + 1) * height)
            cols = slice(c * width, (c + 1) * width)
            y = xp_ref[rows, cols] + scale * _dot(act_ref[slot, rows, :],
                                                  wout_ref[:, cols])
            if final_norm:
                y = _rmsnorm(y, fg_ref[...])
            o_ref[rows, cols] = y
        return [functools.partial(part, r, c)
                for c in range(col_parts) for r in range(row_parts)]

    no_parts = [lambda: None] * (row_parts * col_parts)

    @pl.when(i == 0)
    def _():
        _run_casts(cast_src, cast_dst)
        stage1(x_ref, ins, act_ref, scratch, 0, i, True, no_parts)

    @pl.when(jnp.logical_and(i > 0, i < n_tiles))
    def _():
        _run_casts(cast_src, cast_dst)
        slot = jnp.bitwise_and(i, 1)
        stage1(x_ref, ins, act_ref, scratch, slot, i, False,
               stage2_parts(1 - slot))

    @pl.when(i == n_tiles)
    def _():
        _run_casts(cast_src, cast_dst)
        for part in stage2_parts((n_tiles - 1) % 2):
            part()


def _sublayer_call(name, stage1, x2, w_out, fg, ins, in_specs, act_cols, scratch,
                   casts, *, tm, n_parts, scale, final_norm):
    t, d = x2.shape
    n_tiles = t // tm
    c_in, c_out, c_shapes = _cast_specs(casts, n_tiles + 1)
    kern = functools.partial(
        _sublayer_kernel, stage1=stage1, n_in=len(ins), n_cast=len(casts),
        n_tiles=n_tiles, n_parts=n_parts, scale=scale, final_norm=final_norm)
    outs = pl.pallas_call(
        kern,
        grid=(n_tiles + 1,),
        in_specs=[
            pl.BlockSpec((tm, d), lambda i: (jnp.minimum(i, n_tiles - 1), 0)),
            pl.BlockSpec((tm, d), lambda i: (jnp.maximum(i - 1, 0), 0)),
            _resident(w_out.shape),
            _resident((1, d)),
        ] + in_specs + c_in,
        out_specs=[pl.BlockSpec((tm, d), lambda i: (jnp.maximum(i - 1, 0), 0))
                   ] + c_out,
        out_shape=[jax.ShapeDtypeStruct((t, d), jnp.float32)] + c_shapes,
        scratch_shapes=[pltpu.VMEM((2, tm, act_cols), jnp.bfloat16)] + scratch,
        compiler_params=pltpu.CompilerParams(
            dimension_semantics=("arbitrary",), vmem_limit_bytes=VMEM_LIMIT),
        name=name,
    )(x2, x2, w_out, fg, *ins, *[w for w, _ in casts])
    return outs[0], outs[1:]


FFN_SUBTILES = 2


def _ffn_kernel(*refs, n_cast, final_norm):
    x_ref, g_ref, win_ref, wout_ref, fg_ref = refs[:5]
    cast_src = refs[5:5 + n_cast]
    o_ref = refs[5 + n_cast]
    cast_dst = refs[6 + n_cast:6 + 2 * n_cast]
    xg_ref, a_ref, r_ref = refs[6 + 2 * n_cast:]
    _run_casts(cast_src, cast_dst)
    for r0 in range(0, x_ref.shape[0], TM):
        rows = pl.ds(r0, TM)
        x = x_ref[rows, :]
        xg_ref[rows, :] = (x * g_ref[...]).astype(jnp.bfloat16)
        r = lax.rsqrt(jnp.mean(x * x, axis=-1, keepdims=True) + EPS)
        r_ref[rows, :] = jnp.broadcast_to(r, (TM, MXU_N))
        for c in range(D_FF // MXU_N):
            lo = c * MXU_N
            gate = _dot(xg_ref[rows, :], win_ref[:, lo:lo + MXU_N]) * r_ref[rows, :]
            up = _dot(xg_ref[rows, :],
                      win_ref[:, D_FF + lo:D_FF + lo + MXU_N]) * r_ref[rows, :]
            a_ref[rows, lo:lo + MXU_N] = (
                gate * jax.nn.sigmoid(gate) * up).astype(jnp.bfloat16)
        y = x_ref[rows, :] + 0.5 * _dot(a_ref[rows, :], wout_ref[:, :D_MODEL])
        if final_norm:
            y = _rmsnorm(y, fg_ref[...])
        o_ref[rows, :] = y


def _ffn(x2, g, w_in, w_out, fg, casts, *, final_norm):
    t, d = x2.shape
    tb = FFN_SUBTILES * TM
    n_steps = t // tb
    c_in, c_out, c_shapes = _cast_specs(casts, n_steps)
    kern = functools.partial(_ffn_kernel, n_cast=len(casts), final_norm=final_norm)
    outs = pl.pallas_call(
        kern,
        grid=(n_steps,),
        in_specs=[
            pl.BlockSpec((tb, d), lambda i: (i, 0)),
            _resident((1, d)),
            _resident(w_in.shape),
            _resident(w_out.shape),
            _resident((1, d)),
        ] + c_in,
        out_specs=[pl.BlockSpec((tb, d), lambda i: (i, 0))] + c_out,
        out_shape=[jax.ShapeDtypeStruct((t, d), jnp.float32)] + c_shapes,
        scratch_shapes=[
            pltpu.VMEM((tb, d), jnp.bfloat16),
            pltpu.VMEM((tb, D_FF), jnp.bfloat16),
            pltpu.VMEM((tb, MXU_N), jnp.float32),
        ],
        compiler_params=pltpu.CompilerParams(
            dimension_semantics=("arbitrary",), vmem_limit_bytes=VMEM_LIMIT),
        name="ffn",
    )(x2, g, w_in, w_out, fg, *[w for w, _ in casts])
    return outs[0], outs[1:]


def _gelu_tanh(x):
    k = -2.0 * 0.7978845608028654 * 1.4426950408889634
    return x / (1.0 + jnp.exp2(x * (k + (k * 0.044715) * (x * x))))


def _mixa_stage1(x_ref, ins, act_ref, scratch, slot, step, first, out_parts):
    del step, first
    g_ref, win_ref, vn_ref, ws_ref, bst_ref = ins
    h_ref, u_ref, v_ref = scratch
    tm = x_ref.shape[0]
    n_chunks = 4
    col_chunk = A_DIM // n_chunks
    out_parts = list(out_parts)
    n_after = (len(out_parts) - 1) // (n_chunks - 1)
    out_parts.pop(0)()
    h_ref[...] = _rmsnorm(x_ref[...], g_ref[...]).astype(jnp.bfloat16)
    ssq = jnp.zeros((tm, 1), jnp.float32)
    for c in range(n_chunks):
        lo = c * col_chunk
        v = _gelu_tanh(_dot(h_ref[...], win_ref[:, A_DIM + lo:A_DIM + lo + col_chunk]))
        for _ in range(min(n_after, len(out_parts))):
            out_parts.pop(0)()
        ssq = ssq + jnp.sum(v * v, axis=-1, keepdims=True)
        v_ref[:, lo:lo + col_chunk] = v
    inv = lax.rsqrt(ssq * (1.0 / A_DIM) + EPS)
    row = lax.broadcasted_iota(jnp.int32, (CHUNK, CHUNK), 0)
    col = lax.broadcasted_iota(jnp.int32, (CHUNK, CHUNK), 1)
    causal = col <= row

    def u_proj(hd):
        lo = hd * A_HEAD_DIM
        u_ref[:, lo:lo + A_HEAD_DIM] = _gelu_tanh(
            _dot(h_ref[...], win_ref[:, lo:lo + A_HEAD_DIM]))

    def gate_head(hd):
        lo = hd * A_HEAD_DIM
        ws = jnp.where(causal, ws_ref[hd], 0.0).astype(jnp.bfloat16)
        bias = bst_ref[:, hd:hd + 1]
        vn = (v_ref[:, lo:lo + A_HEAD_DIM] * inv
              * vn_ref[:, lo:lo + A_HEAD_DIM]).astype(jnp.bfloat16)
        for ck in range(tm // CHUNK):
            r0 = ck * CHUNK
            gate = _dot(ws, vn[r0:r0 + CHUNK, :]) + bias
            act_ref[slot, r0:r0 + CHUNK, lo:lo + A_HEAD_DIM] = (
                u_ref[r0:r0 + CHUNK, lo:lo + A_HEAD_DIM] * gate
            ).astype(jnp.bfloat16)

    u_proj(0)
    for hd in range(A_HEADS):
        if hd + 1 < A_HEADS:
            u_proj(hd + 1)
        gate_head(hd)
    assert not out_parts


def _mixa(x2, g, w_in, v_norm, w_s, b_st, w_out, fg, casts):
    d = x2.shape[1]
    return _sublayer_call(
        "mix_sgu", _mixa_stage1, x2, w_out, fg, [g, w_in, v_norm, w_s, b_st],
        [_resident((1, d)), _resident(w_in.shape), _resident((1, A_DIM)),
         _resident(w_s.shape), _resident(b_st.shape)], A_DIM,
        [pltpu.VMEM((TM, d), jnp.bfloat16),
         pltpu.VMEM((TM, A_DIM), jnp.float32),
         pltpu.VMEM((TM, A_DIM), jnp.float32)], casts,
        tm=TM, n_parts=(2, 4), scale=1.0, final_norm=False)


def _mixb_stage1(x_ref, ins, act_ref, scratch, slot, step, first, out_parts, *,
                 seq_tiles):
    g_ref, win_ref, wgrp_ref, scale_ref = ins
    ext_ref, sum_a_ref, sum_b_ref, h_ref = scratch
    tm = x_ref.shape[0]
    rows = HALO + tm
    s = lax.rem(step, seq_tiles)
    if first:
        ext_ref[0:HALO, :] = jnp.zeros((HALO, D_MODEL), jnp.float32)
    else:
        ext_ref[0:HALO, :] = jnp.where(s == 0, 0.0, ext_ref[tm:tm + HALO, :])
    pos = s * tm + lax.broadcasted_iota(jnp.int32, (tm, 1), 0)
    out_parts = list(out_parts)

    def in_proj(gi):
        cols = slice(gi * B_GROUP_DIM, (gi + 1) * B_GROUP_DIM)
        ext_ref[HALO:, cols] = _dot(h_ref[...], win_ref[:, cols])

    def pool_group(gi):
        w = POOL_WINDOWS[gi]
        lo = gi * B_GROUP_DIM
        cols = slice(lo, lo + B_GROUP_DIM)
        src, src_cols, level = ext_ref, cols, 0
        k = 1
        while 2 * k < w:
            level += 1
            r0 = 8 * level
            dst = sum_a_ref if level % 2 else sum_b_ref
            dst[r0:rows, :] = (src[r0:rows, src_cols]
                               + src[r0 - k:rows - k, src_cols])
            src, src_cols = dst, slice(None)
            k *= 2
        acc = src[HALO:rows, src_cols] + src[HALO - k:rows - k, src_cols]
        cnt = jnp.minimum(pos + 1, w).astype(jnp.float32)
        pooled = acc / cnt - ext_ref[HALO:, cols]
        y = _dot(pooled.astype(jnp.bfloat16),
                 wgrp_ref[lo:lo + B_GROUP_DIM, :B_GROUP_DIM])
        act_ref[slot, :, cols] = (y * scale_ref[:, cols]).astype(jnp.bfloat16)

    order = sorted(range(B_GROUPS), key=lambda gi: -POOL_WINDOWS[gi])
    per_slot = len(out_parts) // B_GROUPS

    def run_out_parts():
        for _ in range(min(per_slot, len(out_parts))):
            out_parts.pop(0)()

    run_out_parts()
    h_ref[...] = _rmsnorm(x_ref[...], g_ref[...]).astype(jnp.bfloat16)
    in_proj(order[0])
    for n, gi in enumerate(order):
        if n + 1 < len(order):
            in_proj(order[n + 1])
        run_out_parts()
        pool_group(gi)
    assert not out_parts


def _mixb(x2, seq, g, w_in, w_grp, scale, w_out, fg, casts):
    d = x2.shape[1]
    stage1 = functools.partial(_mixb_stage1, seq_tiles=seq // POOL_TM)
    return _sublayer_call(
        "mix_pool", stage1, x2, w_out, fg, [g, w_in, w_grp, scale],
        [_resident((1, d)), _resident(w_in.shape), _resident(w_grp.shape),
         _resident((1, d))], d,
        [pltpu.VMEM((HALO + POOL_TM, d), jnp.float32),
         pltpu.VMEM((HALO + POOL_TM, B_GROUP_DIM), jnp.float32),
         pltpu.VMEM((HALO + POOL_TM, B_GROUP_DIM), jnp.float32),
         pltpu.VMEM((POOL_TM, d), jnp.bfloat16)], casts,
        tm=POOL_TM, n_parts=(POOL_TM // TM, 4), scale=1.0, final_norm=False)


def _stack3(w):
    return w.reshape((-1,) + w.shape[-2:])


def kernel(x, ffn_norm, ffn_w_in, ffn_w_out, mix_norm, a_w_in, a_v_norm, a_w_s,
           a_b_s, a_w_out, b_w_in, b_w_grp, b_scale, b_w_out, final_norm):
    bsz, seq, d = x.shape
    depth = ffn_norm.shape[0]
    row = lambda v: v.reshape(1, -1)
    fw_in, fw_out = _stack3(ffn_w_in), _stack3(ffn_w_out)
    bw_grp = b_w_grp.reshape(b_w_grp.shape[0], -1, b_w_grp.shape[-1])

    subs = []
    for i in range(depth):
        subs.append(("ffn", i, 0, [(fw_in, 2 * i), (fw_out, 2 * i)]))
        j = i // 2
        if i % 2 == 0:
            subs.append(("sgu", i, j, [(a_w_in, j), (a_w_out, j)]))
        else:
            subs.append(("pool", i, j, [(b_w_in, j), (bw_grp, j), (b_w_out, j)]))
        subs.append(("ffn", i, 1, [(fw_in, 2 * i + 1), (fw_out, 2 * i + 1)]))

    x2 = x.reshape(bsz * seq, d)
    fg = row(final_norm)
    weights = [_cast_padded(w[l]) for w, l in subs[0][3]]
    for n, (kind, i, k, _) in enumerate(subs):
        casts = subs[n + 1][3] if n + 1 < len(subs) else []
        if kind == "ffn":
            x2, nxt = _ffn(x2, row(ffn_norm[i, k]), weights[0], weights[1], fg,
                           casts, final_norm=(n == len(subs) - 1))
        elif kind == "sgu":
            x2, nxt = _mixa(x2, row(mix_norm[i]), weights[0], row(a_v_norm[k]),
                            a_w_s[k], a_b_s[k].T, weights[1], fg, casts)
        else:
            x2, nxt = _mixb(x2, seq, row(mix_norm[i]), weights[0], weights[1],
                            row(b_scale[k]), weights[2], fg, casts)
        weights = list(nxt)
    return x2.reshape(bsz, seq, d)
```

```python
import functools

import jax
import jax.numpy as jnp
from jax import lax
from jax.experimental import pallas as pl
from jax.experimental.pallas import tpu as pltpu

D_MODEL = 1024
D_FF = 2816
CHUNK = 128
A_DIM = 2 * D_MODEL
A_HEADS = 8
A_HEAD_DIM = A_DIM // A_HEADS
POOL_WINDOWS = (2, 4, 8, 16)
B_GROUPS = len(POOL_WINDOWS)
B_GROUP_DIM = D_MODEL // B_GROUPS
EPS = 1e-6

HALO = 32
MXU_N = 256
LANES = 128
BF16_SUBLANES = 16
VMEM_LIMIT = 56 * 1024 * 1024

TM = 512


def _rmsnorm(x, g):
    ms = jnp.mean(x * x, axis=-1, keepdims=True)
    return x * lax.rsqrt(ms + EPS) * g


def _dot(a, b):
    return jnp.dot(a, b, preferred_element_type=jnp.float32)


def _resident(shape):
    return pl.BlockSpec(shape, lambda *_: (0,) * len(shape),
                        pipeline_mode=pl.Buffered(1))


def _cast_blocks(rows, n_steps):
    best = 1
    for nb in range(1, n_steps + 1):
        if rows % nb == 0 and (rows // nb) % BF16_SUBLANES == 0:
            best = nb
    return best


def _padded_cols(cols):
    return cols + LANES if (cols // LANES) % 8 == 0 else cols


def _cast_padded(w):
    pad = _padded_cols(w.shape[-1]) - w.shape[-1]
    wb = w.astype(jnp.bfloat16)
    if pad == 0:
        return wb
    return jnp.concatenate([wb, jnp.zeros((w.shape[0], pad), jnp.bfloat16)], axis=1)


def _cast_specs(casts, n_steps):
    in_specs, out_specs, out_shapes = [], [], []
    for w, layer in casts:
        _, rows, cols = w.shape
        nb = _cast_blocks(rows, n_steps)
        rb = rows // nb
        in_specs.append(pl.BlockSpec(
            (None, rb, cols),
            lambda i, layer=layer, nb=nb: (layer, jnp.minimum(i, nb - 1), 0)))
        out_specs.append(pl.BlockSpec(
            (rb, _padded_cols(cols)),
            lambda i, nb=nb: (jnp.minimum(i, nb - 1), 0)))
        out_shapes.append(
            jax.ShapeDtypeStruct((rows, _padded_cols(cols)), jnp.bfloat16))
    return in_specs, out_specs, out_shapes


def _run_casts(src_refs, dst_refs):
    for src, dst in zip(src_refs, dst_refs):
        rows, cols = src.shape
        dst[:, :cols] = src[...].astype(jnp.bfloat16)
        if dst.shape[1] > cols:
            dst[:, cols:] = jnp.zeros((rows, dst.shape[1] - cols), jnp.bfloat16)


def _sublayer_kernel(*refs, stage1, n_in, n_cast, n_tiles, n_parts, scale,
                     final_norm):
    assert n_parts[1] == 1 or not final_norm
    x_ref, xp_ref, wout_ref, fg_ref = refs[:4]
    ins = refs[4:4 + n_in]
    cast_src = refs[4 + n_in:4 + n_in + n_cast]
    o_ref = refs[4 + n_in + n_cast]
    cast_dst = refs[5 + n_in + n_cast:5 + n_in + 2 * n_cast]
    act_ref = refs[5 + n_in + 2 * n_cast]
    scratch = refs[6 + n_in + 2 * n_cast:]
    i = pl.program_id(0)

    row_parts, col_parts = n_parts

    def stage2_parts(slot):
        def part(r, c):
            height, width = x_ref.shape[0] // row_parts, D_MODEL // col_parts
            rows = slice(r * height, (r + 1) * height)
            cols = slice(c * width, (c + 1) * width)
            y = xp_ref[rows, cols] + scale * _dot(act_ref[slot, rows, :],
                                                  wout_ref[:, cols])
            if final_norm:
                y = _rmsnorm(y, fg_ref[...])
            o_ref[rows, cols] = y
        return [functools.partial(part, r, c)
                for c in range(col_parts) for r in range(row_parts)]

    no_parts = [lambda: None] * (row_parts * col_parts)

    @pl.when(i == 0)
    def _():
        _run_casts(cast_src, cast_dst)
        stage1(x_ref, ins, act_ref, scratch, 0, i, True, no_parts)

    @pl.when(jnp.logical_and(i > 0, i < n_tiles))
    def _():
        _run_casts(cast_src, cast_dst)
        slot = jnp.bitwise_and(i, 1)
        stage1(x_ref, ins, act_ref, scratch, slot, i, False,
               stage2_parts(1 - slot))

    @pl.when(i == n_tiles)
    def _():
        _run_casts(cast_src, cast_dst)
        for part in stage2_parts((n_tiles - 1) % 2):
            part()


def _sublayer_call(name, stage1, x2, w_out, fg, ins, in_specs, act_cols, scratch,
                   casts, *, tm, n_parts, scale, final_norm):
    t, d = x2.shape
    n_tiles = t // tm
    c_in, c_out, c_shapes = _cast_specs(casts, n_tiles + 1)
    kern = functools.partial(
        _sublayer_kernel, stage1=stage1, n_in=len(ins), n_cast=len(casts),
        n_tiles=n_tiles, n_parts=n_parts, scale=scale, final_norm=final_norm)
    outs = pl.pallas_call(
        kern,
        grid=(n_tiles + 1,),
        in_specs=[
            pl.BlockSpec((tm, d), lambda i: (jnp.minimum(i, n_tiles - 1), 0)),
            pl.BlockSpec((tm, d), lambda i: (jnp.maximum(i - 1, 0), 0)),
            _resident(w_out.shape),
            _resident((1, d)),
        ] + in_specs + c_in,
        out_specs=[pl.BlockSpec((tm, d), lambda i: (jnp.maximum(i - 1, 0), 0))
                   ] + c_out,
        out_shape=[jax.ShapeDtypeStruct((t, d), jnp.float32)] + c_shapes,
        scratch_shapes=[pltpu.VMEM((2, tm, act_cols), jnp.bfloat16)] + scratch,
        compiler_params=pltpu.CompilerParams(
            dimension_semantics=("arbitrary",), vmem_limit_bytes=VMEM_LIMIT),
        name=name,
    )(x2, x2, w_out, fg, *ins, *[w for w, _ in casts])
    return outs[0], outs[1:]


FFN_SUBTILES = 2


def _ffn_kernel(*refs, n_cast, final_norm):
    x_ref, g_ref, win_ref, wout_ref, fg_ref = refs[:5]
    cast_src = refs[5:5 + n_cast]
    o_ref = refs[5 + n_cast]
    cast_dst = refs[6 + n_cast:6 + 2 * n_cast]
    xg_ref, a_ref, r_ref = refs[6 + 2 * n_cast:]
    _run_casts(cast_src, cast_dst)
    for r0 in range(0, x_ref.shape[0], TM):
        rows = pl.ds(r0, TM)
        x = x_ref[rows, :]
        xg_ref[rows, :] = (x * g_ref[...]).astype(jnp.bfloat16)
        r = lax.rsqrt(jnp.mean(x * x, axis=-1, keepdims=True) + EPS)
        r_ref[rows, :] = jnp.broadcast_to(r, (TM, MXU_N))
        for c in range(D_FF // MXU_N):
            lo = c * MXU_N
            gate = _dot(xg_ref[rows, :], win_ref[:, lo:lo + MXU_N]) * r_ref[rows, :]
            up = _dot(xg_ref[rows, :],
                      win_ref[:, D_FF + lo:D_FF + lo + MXU_N]) * r_ref[rows, :]
            a_ref[rows, lo:lo + MXU_N] = (
                gate * jax.nn.sigmoid(gate) * up).astype(jnp.bfloat16)
        y = x_ref[rows, :] + 0.5 * _dot(a_ref[rows, :], wout_ref[:, :D_MODEL])
        if final_norm:
            y = _rmsnorm(y, fg_ref[...])
        o_ref[rows, :] = y


def _ffn(x2, g, w_in, w_out, fg, casts, *, final_norm):
    t, d = x2.shape
    tb = FFN_SUBTILES * TM
    n_steps = t // tb
    c_in, c_out, c_shapes = _cast_specs(casts, n_steps)
    kern = functools.partial(_ffn_kernel, n_cast=len(casts), final_norm=final_norm)
    outs = pl.pallas_call(
        kern,
        grid=(n_steps,),
        in_specs=[
            pl.BlockSpec((tb, d), lambda i: (i, 0)),
            _resident((1, d)),
            _resident(w_in.shape),
            _resident(w_out.shape),
            _resident((1, d)),
        ] + c_in,
        out_specs=[pl.BlockSpec((tb, d), lambda i: (i, 0))] + c_out,
        out_shape=[jax.ShapeDtypeStruct((t, d), jnp.float32)] + c_shapes,
        scratch_shapes=[
            pltpu.VMEM((tb, d), jnp.bfloat16),
            pltpu.VMEM((tb, D_FF), jnp.bfloat16),
            pltpu.VMEM((tb, MXU_N), jnp.float32),
        ],
        compiler_params=pltpu.CompilerParams(
            dimension_semantics=("arbitrary",), vmem_limit_bytes=VMEM_LIMIT),
        name="ffn",
    )(x2, g, w_in, w_out, fg, *[w for w, _ in casts])
    return outs[0], outs[1:]


def _gelu_tanh(x):
    k = -2.0 * 0.7978845608028654 * 1.4426950408889634
    return x / (1.0 + jnp.exp2(x * (k + (k * 0.044715) * (x * x))))


def _mixa_stage1(x_ref, ins, act_ref, scratch, slot, step, first, out_parts):
    del step, first
    g_ref, win_ref, vn_ref, ws_ref, bst_ref = ins
    h_ref, u_ref, v_ref = scratch
    tm = x_ref.shape[0]
    n_chunks = 4
    col_chunk = A_DIM // n_chunks
    out_parts = list(out_parts)
    n_after = (len(out_parts) - 1) // (n_chunks - 1)
    out_parts.pop(0)()
    h_ref[...] = _rmsnorm(x_ref[...], g_ref[...]).astype(jnp.bfloat16)
    ssq = jnp.zeros((tm, 1), jnp.float32)
    for c in range(n_chunks):
        lo = c * col_chunk
        v = _gelu_tanh(_dot(h_ref[...], win_ref[:, A_DIM + lo:A_DIM + lo + col_chunk]))
        for _ in range(min(n_after, len(out_parts))):
            out_parts.pop(0)()
        ssq = ssq + jnp.sum(v * v, axis=-1, keepdims=True)
        v_ref[:, lo:lo + col_chunk] = v
    inv = lax.rsqrt(ssq * (1.0 / A_DIM) + EPS)
    row = lax.broadcasted_iota(jnp.int32, (CHUNK, CHUNK), 0)
    col = lax.broadcasted_iota(jnp.int32, (CHUNK, CHUNK), 1)
    causal = col <= row

    def u_proj(hd):
        lo = hd * A_HEAD_DIM
        u_ref[:, lo:lo + A_HEAD_DIM] = _gelu_tanh(
            _dot(h_ref[...], win_ref[:, lo:lo + A_HEAD_DIM]))

    def gate_head(hd):
        lo = hd * A_HEAD_DIM
        ws = jnp.where(causal, ws_ref[hd], 0.0).astype(jnp.bfloat16)
        bias = bst_ref[:, hd:hd + 1]
        vn = (v_ref[:, lo:lo + A_HEAD_DIM] * inv
              * vn_ref[:, lo:lo + A_HEAD_DIM]).astype(jnp.bfloat16)
        for ck in range(tm // CHUNK):
            r0 = ck * CHUNK
            gate = _dot(ws, vn[r0:r0 + CHUNK, :]) + bias
            act_ref[slot, r0:r0 + CHUNK, lo:lo + A_HEAD_DIM] = (
                u_ref[r0:r0 + CHUNK, lo:lo + A_HEAD_DIM] * gate
            ).astype(jnp.bfloat16)

    u_proj(0)
    for hd in range(A_HEADS):
        if hd + 1 < A_HEADS:
            u_proj(hd + 1)
        gate_head(hd)
    assert not out_parts


def _mixa(x2, g, w_in, v_norm, w_s, b_st, w_out, fg, casts):
    d = x2.shape[1]
    return _sublayer_call(
        "mix_sgu", _mixa_stage1, x2, w_out, fg, [g, w_in, v_norm, w_s, b_st],
        [_resident((1, d)), _resident(w_in.shape), _resident((1, A_DIM)),
         _resident(w_s.shape), _resident(b_st.shape)], A_DIM,
        [pltpu.VMEM((TM, d), jnp.bfloat16),
         pltpu.VMEM((TM, A_DIM), jnp.float32),
         pltpu.VMEM((TM, A_DIM), jnp.float32)], casts,
        tm=TM, n_parts=(2, 4), scale=1.0, final_norm=False)


POOL_SUBTILES = 2


def _mixb_kernel(*refs, n_cast, seq_steps):
    x_ref, g_ref, win_ref, wgrp_ref, scale_ref, wout_ref = refs[:6]
    cast_src = refs[6:6 + n_cast]
    o_ref = refs[6 + n_cast]
    cast_dst = refs[7 + n_cast:7 + 2 * n_cast]
    ext_ref, sum_a_ref, sum_b_ref, xg_ref, r_ref = refs[7 + 2 * n_cast:]
    i = pl.program_id(0)
    rows_ext = HALO + TM

    @pl.when(i == 0)
    def _():
        ext_ref[TM:rows_ext, :] = jnp.zeros((HALO, D_MODEL), jnp.float32)

    _run_casts(cast_src, cast_dst)
    step_in_seq = lax.rem(i, seq_steps)
    order = sorted(range(B_GROUPS), key=lambda gi: -POOL_WINDOWS[gi])

    for sub in range(POOL_SUBTILES):
        rows = pl.ds(sub * TM, TM)
        x = x_ref[rows, :]
        xg_ref[...] = (x * g_ref[...]).astype(jnp.bfloat16)
        r = lax.rsqrt(jnp.mean(x * x, axis=-1, keepdims=True) + EPS)
        r_ref[...] = jnp.broadcast_to(r, (TM, B_GROUP_DIM))
        carry = ext_ref[TM:rows_ext, :]
        if sub == 0:
            carry = jnp.where(step_in_seq == 0, 0.0, carry)
        ext_ref[0:HALO, :] = carry
        pos = ((step_in_seq * POOL_SUBTILES + sub) * TM
               + lax.broadcasted_iota(jnp.int32, (TM, 1), 0))

        def in_proj(gi):
            cols = slice(gi * B_GROUP_DIM, (gi + 1) * B_GROUP_DIM)
            ext_ref[HALO:, cols] = _dot(xg_ref[...], win_ref[:, cols]) * r_ref[...]

        def pool_group(gi, first):
            w = POOL_WINDOWS[gi]
            lo = gi * B_GROUP_DIM
            cols = slice(lo, lo + B_GROUP_DIM)
            src, src_cols, level = ext_ref, cols, 0
            k = 1
            while 2 * k < w:
                level += 1
                r0 = 8 * level
                dst = sum_a_ref if level % 2 else sum_b_ref
                dst[r0:rows_ext, :] = (src[r0:rows_ext, src_cols]
                                       + src[r0 - k:rows_ext - k, src_cols])
                src, src_cols = dst, slice(None)
                k *= 2
            acc = src[HALO:rows_ext, src_cols] + src[HALO - k:rows_ext - k, src_cols]
            cnt = jnp.minimum(pos + 1, w).astype(jnp.float32)
            pooled = acc / cnt - ext_ref[HALO:, cols]
            y = _dot(pooled.astype(jnp.bfloat16),
                     wgrp_ref[lo:lo + B_GROUP_DIM, :B_GROUP_DIM])
            act = (y * scale_ref[:, cols]).astype(jnp.bfloat16)
            part = _dot(act, wout_ref[lo:lo + B_GROUP_DIM, :D_MODEL])
            if first:
                o_ref[rows, :] = x_ref[rows, :] + part
            else:
                o_ref[rows, :] += part

        for gi in order:
            in_proj(gi)
        for n, gi in enumerate(order):
            pool_group(gi, first=(n == 0))


def _mixb(x2, seq, g, w_in, w_grp, scale, w_out, casts):
    t, d = x2.shape
    tb = POOL_SUBTILES * TM
    n_steps = t // tb
    c_in, c_out, c_shapes = _cast_specs(casts, n_steps)
    kern = functools.partial(_mixb_kernel, n_cast=len(casts), seq_steps=seq // tb)
    outs = pl.pallas_call(
        kern,
        grid=(n_steps,),
        in_specs=[
            pl.BlockSpec((tb, d), lambda i: (i, 0)),
            _resident((1, d)),
            _resident(w_in.shape),
            _resident(w_grp.shape),
            _resident((1, d)),
            _resident(w_out.shape),
        ] + c_in,
        out_specs=[pl.BlockSpec((tb, d), lambda i: (i, 0))] + c_out,
        out_shape=[jax.ShapeDtypeStruct((t, d), jnp.float32)] + c_shapes,
        scratch_shapes=[
            pltpu.VMEM((HALO + TM, d), jnp.float32),
            pltpu.VMEM((HALO + TM, B_GROUP_DIM), jnp.float32),
            pltpu.VMEM((HALO + TM, B_GROUP_DIM), jnp.float32),
            pltpu.VMEM((TM, d), jnp.bfloat16),
            pltpu.VMEM((TM, B_GROUP_DIM), jnp.float32),
        ],
        compiler_params=pltpu.CompilerParams(
            dimension_semantics=("arbitrary",), vmem_limit_bytes=VMEM_LIMIT),
        name="mix_pool",
    )(x2, g, w_in, w_grp, scale, w_out, *[w for w, _ in casts])
    return outs[0], outs[1:]


def _stack3(w):
    return w.reshape((-1,) + w.shape[-2:])


def kernel(x, ffn_norm, ffn_w_in, ffn_w_out, mix_norm, a_w_in, a_v_norm, a_w_s,
           a_b_s, a_w_out, b_w_in, b_w_grp, b_scale, b_w_out, final_norm):
    bsz, seq, d = x.shape
    depth = ffn_norm.shape[0]
    row = lambda v: v.reshape(1, -1)
    fw_in, fw_out = _stack3(ffn_w_in), _stack3(ffn_w_out)
    bw_grp = b_w_grp.reshape(b_w_grp.shape[0], -1, b_w_grp.shape[-1])

    subs = []
    for i in range(depth):
        subs.append(("ffn", i, 0, [(fw_in, 2 * i), (fw_out, 2 * i)]))
        j = i // 2
        if i % 2 == 0:
            subs.append(("sgu", i, j, [(a_w_in, j), (a_w_out, j)]))
        else:
            subs.append(("pool", i, j, [(b_w_in, j), (bw_grp, j), (b_w_out, j)]))
        subs.append(("ffn", i, 1, [(fw_in, 2 * i + 1), (fw_out, 2 * i + 1)]))

    x2 = x.reshape(bsz * seq, d)
    fg = row(final_norm)
    weights = [_cast_padded(w[l]) for w, l in subs[0][3]]
    for n, (kind, i, k, _) in enumerate(subs):
        casts = subs[n + 1][3] if n + 1 < len(subs) else []
        if kind == "ffn":
            x2, nxt = _ffn(x2, row(ffn_norm[i, k]), weights[0], weights[1], fg,
                           casts, final_norm=(n == len(subs) - 1))
        elif kind == "sgu":
            x2, nxt = _mixa(x2, row(mix_norm[i]), weights[0], row(a_v_norm[k]),
                            a_w_s[k], a_b_s[k].T, weights[1], fg, casts)
        else:
            x2, nxt = _mixb(x2, seq, row(mix_norm[i]), weights[0], weights[1],
                            row(b_scale[k]), weights[2], casts)
        weights = list(nxt)
    return x2.reshape(bsz, seq, d)
```

```python
import functools

import jax
import jax.numpy as jnp
from jax import lax
from jax.experimental import pallas as pl
from jax.experimental.pallas import tpu as pltpu

D_MODEL = 1024
D_FF = 2816
CHUNK = 128
A_DIM = 2 * D_MODEL
A_HEADS = 8
A_HEAD_DIM = A_DIM // A_HEADS
POOL_WINDOWS = (2, 4, 8, 16)
B_GROUPS = len(POOL_WINDOWS)
B_GROUP_DIM = D_MODEL // B_GROUPS
EPS = 1e-6

HALO = 32
MXU_N = 256
LANES = 128
BF16_SUBLANES = 16
VMEM_LIMIT = 56 * 1024 * 1024

TM = 512


def _rmsnorm(x, g):
    ms = jnp.mean(x * x, axis=-1, keepdims=True)
    return x * lax.rsqrt(ms + EPS) * g


def _dot(a, b):
    return jnp.dot(a, b, preferred_element_type=jnp.float32)


def _resident(shape):
    return pl.BlockSpec(shape, lambda *_: (0,) * len(shape),
                        pipeline_mode=pl.Buffered(1))


def _cast_blocks(rows, n_steps):
    best = 1
    for nb in range(1, n_steps + 1):
        if rows % nb == 0 and (rows // nb) % BF16_SUBLANES == 0:
            best = nb
    return best


def _padded_cols(cols):
    return cols + LANES if (cols // LANES) % 8 == 0 else cols


def _cast_padded(w):
    pad = _padded_cols(w.shape[-1]) - w.shape[-1]
    wb = w.astype(jnp.bfloat16)
    if pad == 0:
        return wb
    return jnp.concatenate([wb, jnp.zeros((w.shape[0], pad), jnp.bfloat16)], axis=1)


def _cast_specs(casts, n_steps):
    in_specs, out_specs, out_shapes = [], [], []
    for w, layer in casts:
        _, rows, cols = w.shape
        nb = _cast_blocks(rows, n_steps)
        rb = rows // nb
        in_specs.append(pl.BlockSpec(
            (None, rb, cols),
            lambda i, layer=layer, nb=nb: (layer, jnp.minimum(i, nb - 1), 0)))
        out_specs.append(pl.BlockSpec(
            (rb, _padded_cols(cols)),
            lambda i, nb=nb: (jnp.minimum(i, nb - 1), 0)))
        out_shapes.append(
            jax.ShapeDtypeStruct((rows, _padded_cols(cols)), jnp.bfloat16))
    return in_specs, out_specs, out_shapes


def _run_casts(src_refs, dst_refs):
    for src, dst in zip(src_refs, dst_refs):
        rows, cols = src.shape
        dst[:, :cols] = src[...].astype(jnp.bfloat16)
        if dst.shape[1] > cols:
            dst[:, cols:] = jnp.zeros((rows, dst.shape[1] - cols), jnp.bfloat16)


def _sublayer_kernel(*refs, stage1, n_in, n_cast, n_tiles, n_parts, scale,
                     final_norm):
    assert n_parts[1] == 1 or not final_norm
    x_ref, xp_ref, wout_ref, fg_ref = refs[:4]
    ins = refs[4:4 + n_in]
    cast_src = refs[4 + n_in:4 + n_in + n_cast]
    o_ref = refs[4 + n_in + n_cast]
    cast_dst = refs[5 + n_in + n_cast:5 + n_in + 2 * n_cast]
    act_ref = refs[5 + n_in + 2 * n_cast]
    scratch = refs[6 + n_in + 2 * n_cast:]
    i = pl.program_id(0)

    row_parts, col_parts = n_parts

    def stage2_parts(slot):
        def part(r, c):
            height, width = x_ref.shape[0] // row_parts, D_MODEL // col_parts
            rows = slice(r * height, (r + 1) * height)
            cols = slice(c * width, (c + 1) * width)
            y = xp_ref[rows, cols] + scale * _dot(act_ref[slot, rows, :],
                                                  wout_ref[:, cols])
            if final_norm:
                y = _rmsnorm(y, fg_ref[...])
            o_ref[rows, cols] = y
        return [functools.partial(part, r, c)
                for c in range(col_parts) for r in range(row_parts)]

    no_parts = [lambda: None] * (row_parts * col_parts)

    @pl.when(i == 0)
    def _():
        _run_casts(cast_src, cast_dst)
        stage1(x_ref, ins, act_ref, scratch, 0, i, True, no_parts)

    @pl.when(jnp.logical_and(i > 0, i < n_tiles))
    def _():
        _run_casts(cast_src, cast_dst)
        slot = jnp.bitwise_and(i, 1)
        stage1(x_ref, ins, act_ref, scratch, slot, i, False,
               stage2_parts(1 - slot))

    @pl.when(i == n_tiles)
    def _():
        _run_casts(cast_src, cast_dst)
        for part in stage2_parts((n_tiles - 1) % 2):
            part()


def _sublayer_call(name, stage1, x2, w_out, fg, ins, in_specs, act_cols, scratch,
                   casts, *, tm, n_parts, scale, final_norm):
    t, d = x2.shape
    n_tiles = t // tm
    c_in, c_out, c_shapes = _cast_specs(casts, n_tiles + 1)
    kern = functools.partial(
        _sublayer_kernel, stage1=stage1, n_in=len(ins), n_cast=len(casts),
        n_tiles=n_tiles, n_parts=n_parts, scale=scale, final_norm=final_norm)
    outs = pl.pallas_call(
        kern,
        grid=(n_tiles + 1,),
        in_specs=[
            pl.BlockSpec((tm, d), lambda i: (jnp.minimum(i, n_tiles - 1), 0)),
            pl.BlockSpec((tm, d), lambda i: (jnp.maximum(i - 1, 0), 0)),
            _resident(w_out.shape),
            _resident((1, d)),
        ] + in_specs + c_in,
        out_specs=[pl.BlockSpec((tm, d), lambda i: (jnp.maximum(i - 1, 0), 0))
                   ] + c_out,
        out_shape=[jax.ShapeDtypeStruct((t, d), jnp.float32)] + c_shapes,
        scratch_shapes=[pltpu.VMEM((2, tm, act_cols), jnp.bfloat16)] + scratch,
        compiler_params=pltpu.CompilerParams(
            dimension_semantics=("arbitrary",), vmem_limit_bytes=VMEM_LIMIT),
        name=name,
    )(x2, x2, w_out, fg, *ins, *[w for w, _ in casts])
    return outs[0], outs[1:]


FFN_SUBTILES = 2


def _ffn_kernel(*refs, n_cast, final_norm):
    x_ref, g_ref, win_ref, wout_ref, fg_ref = refs[:5]
    cast_src = refs[5:5 + n_cast]
    o_ref = refs[5 + n_cast]
    cast_dst = refs[6 + n_cast:6 + 2 * n_cast]
    xg_ref, a_ref, r_ref = refs[6 + 2 * n_cast:]
    _run_casts(cast_src, cast_dst)
    for r0 in range(0, x_ref.shape[0], TM):
        rows = pl.ds(r0, TM)
        x = x_ref[rows, :]
        xg_ref[rows, :] = (x * g_ref[...]).astype(jnp.bfloat16)
        r = lax.rsqrt(jnp.mean(x * x, axis=-1, keepdims=True) + EPS)
        r_ref[rows, :] = jnp.broadcast_to(r, (TM, MXU_N))
        for c in range(D_FF // MXU_N):
            lo = c * MXU_N
            gate = _dot(xg_ref[rows, :], win_ref[:, lo:lo + MXU_N]) * r_ref[rows, :]
            up = _dot(xg_ref[rows, :],
                      win_ref[:, D_FF + lo:D_FF + lo + MXU_N]) * r_ref[rows, :]
            a_ref[rows, lo:lo + MXU_N] = (
                gate * jax.nn.sigmoid(gate) * up).astype(jnp.bfloat16)
        y = x_ref[rows, :] + 0.5 * _dot(a_ref[rows, :], wout_ref[:, :D_MODEL])
        if final_norm:
            y = _rmsnorm(y, fg_ref[...])
        o_ref[rows, :] = y


def _ffn(x2, g, w_in, w_out, fg, casts, *, final_norm):
    t, d = x2.shape
    tb = FFN_SUBTILES * TM
    n_steps = t // tb
    c_in, c_out, c_shapes = _cast_specs(casts, n_steps)
    kern = functools.partial(_ffn_kernel, n_cast=len(casts), final_norm=final_norm)
    outs = pl.pallas_call(
        kern,
        grid=(n_steps,),
        in_specs=[
            pl.BlockSpec((tb, d), lambda i: (i, 0)),
            _resident((1, d)),
            _resident(w_in.shape),
            _resident(w_out.shape),
            _resident((1, d)),
        ] + c_in,
        out_specs=[pl.BlockSpec((tb, d), lambda i: (i, 0))] + c_out,
        out_shape=[jax.ShapeDtypeStruct((t, d), jnp.float32)] + c_shapes,
        scratch_shapes=[
            pltpu.VMEM((tb, d), jnp.bfloat16),
            pltpu.VMEM((tb, D_FF), jnp.bfloat16),
            pltpu.VMEM((tb, MXU_N), jnp.float32),
        ],
        compiler_params=pltpu.CompilerParams(
            dimension_semantics=("arbitrary",), vmem_limit_bytes=VMEM_LIMIT),
        name="ffn",
    )(x2, g, w_in, w_out, fg, *[w for w, _ in casts])
    return outs[0], outs[1:]


def _gelu_tanh(x):
    k = -2.0 * 0.7978845608028654 * 1.4426950408889634
    return x / (1.0 + jnp.exp2(x * (k + (k * 0.044715) * (x * x))))


def _mixa_stage1(x_ref, ins, act_ref, scratch, slot, step, first, out_parts):
    del step, first
    g_ref, win_ref, vn_ref, ws_ref, bst_ref = ins
    h_ref, u_ref, v_ref = scratch
    tm = x_ref.shape[0]
    n_chunks = 4
    col_chunk = A_DIM // n_chunks
    out_parts = list(out_parts)
    n_after = (len(out_parts) - 1) // (n_chunks - 1)
    out_parts.pop(0)()
    h_ref[...] = _rmsnorm(x_ref[...], g_ref[...]).astype(jnp.bfloat16)
    ssq = jnp.zeros((tm, 1), jnp.float32)
    for c in range(n_chunks):
        lo = c * col_chunk
        v = _gelu_tanh(_dot(h_ref[...], win_ref[:, A_DIM + lo:A_DIM + lo + col_chunk]))
        for _ in range(min(n_after, len(out_parts))):
            out_parts.pop(0)()
        ssq = ssq + jnp.sum(v * v, axis=-1, keepdims=True)
        v_ref[:, lo:lo + col_chunk] = v
    inv = lax.rsqrt(ssq * (1.0 / A_DIM) + EPS)
    row = lax.broadcasted_iota(jnp.int32, (CHUNK, CHUNK), 0)
    col = lax.broadcasted_iota(jnp.int32, (CHUNK, CHUNK), 1)
    causal = col <= row

    def u_proj(hd):
        lo = hd * A_HEAD_DIM
        u_ref[:, lo:lo + A_HEAD_DIM] = _gelu_tanh(
            _dot(h_ref[...], win_ref[:, lo:lo + A_HEAD_DIM]))

    def gate_head(hd):
        lo = hd * A_HEAD_DIM
        ws = jnp.where(causal, ws_ref[hd], 0.0).astype(jnp.bfloat16)
        bias = bst_ref[:, hd:hd + 1]
        vn = (v_ref[:, lo:lo + A_HEAD_DIM] * inv
              * vn_ref[:, lo:lo + A_HEAD_DIM]).astype(jnp.bfloat16)
        for ck in range(tm // CHUNK):
            r0 = ck * CHUNK
            gate = _dot(ws, vn[r0:r0 + CHUNK, :]) + bias
            act_ref[slot, r0:r0 + CHUNK, lo:lo + A_HEAD_DIM] = (
                u_ref[r0:r0 + CHUNK, lo:lo + A_HEAD_DIM] * gate
            ).astype(jnp.bfloat16)

    u_proj(0)
    for hd in range(A_HEADS):
        if hd + 1 < A_HEADS:
            u_proj(hd + 1)
        gate_head(hd)
    assert not out_parts


def _mixa(x2, g, w_in, v_norm, w_s, b_st, w_out, fg, casts):
    d = x2.shape[1]
    return _sublayer_call(
        "mix_sgu", _mixa_stage1, x2, w_out, fg, [g, w_in, v_norm, w_s, b_st],
        [_resident((1, d)), _resident(w_in.shape), _resident((1, A_DIM)),
         _resident(w_s.shape), _resident(b_st.shape)], A_DIM,
        [pltpu.VMEM((TM, d), jnp.bfloat16),
         pltpu.VMEM((TM, A_DIM), jnp.float32),
         pltpu.VMEM((TM, A_DIM), jnp.float32)], casts,
        tm=TM, n_parts=(2, 4), scale=1.0, final_norm=False)


POOL_SUBTILES = 2


def _mixb_kernel(*refs, n_cast, seq_steps):
    x_ref, g_ref, win_ref, wgrp_ref, scale_ref, wout_ref = refs[:6]
    cast_src = refs[6:6 + n_cast]
    o_ref = refs[6 + n_cast]
    cast_dst = refs[7 + n_cast:7 + 2 * n_cast]
    ext_ref, sum_a_ref, sum_b_ref, xg_ref, r_ref, act_ref = refs[7 + 2 * n_cast:]
    i = pl.program_id(0)
    rows_ext = HALO + TM
    last = POOL_SUBTILES - 1

    @pl.when(i == 0)
    def _():
        ext_ref[last, TM:rows_ext, :] = jnp.zeros((HALO, D_MODEL), jnp.float32)

    _run_casts(cast_src, cast_dst)
    step_in_seq = lax.rem(i, seq_steps)
    order = sorted(range(B_GROUPS), key=lambda gi: -POOL_WINDOWS[gi])
    halves = (order[:B_GROUPS // 2], order[B_GROUPS // 2:])

    def rows_of(sub):
        return pl.ds(sub * TM, TM)

    def in_proj(sub):
        x = x_ref[rows_of(sub), :]
        xg_ref[sub] = (x * g_ref[...]).astype(jnp.bfloat16)
        r = lax.rsqrt(jnp.mean(x * x, axis=-1, keepdims=True) + EPS)
        r_ref[sub] = jnp.broadcast_to(r, (TM, B_GROUP_DIM))
        carry = ext_ref[(sub - 1) % POOL_SUBTILES, TM:rows_ext, :]
        if sub == 0:
            carry = jnp.where(step_in_seq == 0, 0.0, carry)
        ext_ref[sub, 0:HALO, :] = carry
        for gi in order:
            cols = slice(gi * B_GROUP_DIM, (gi + 1) * B_GROUP_DIM)
            ext_ref[sub, HALO:, cols] = (
                _dot(xg_ref[sub], win_ref[:, cols]) * r_ref[sub])

    def pool_group(sub, gi):
        w = POOL_WINDOWS[gi]
        lo = gi * B_GROUP_DIM
        cols = slice(lo, lo + B_GROUP_DIM)
        pos = ((step_in_seq * POOL_SUBTILES + sub) * TM
               + lax.broadcasted_iota(jnp.int32, (TM, 1), 0))
        src, src_cols, level = ext_ref.at[sub], cols, 0
        k = 1
        while 2 * k < w:
            level += 1
            r0 = 8 * level
            dst = sum_a_ref if level % 2 else sum_b_ref
            dst[r0:rows_ext, :] = (src[r0:rows_ext, src_cols]
                                   + src[r0 - k:rows_ext - k, src_cols])
            src, src_cols = dst, slice(None)
            k *= 2
        acc = src[HALO:rows_ext, src_cols] + src[HALO - k:rows_ext - k, src_cols]
        cnt = jnp.minimum(pos + 1, w).astype(jnp.float32)
        pooled = acc / cnt - ext_ref[sub, HALO:, cols]
        y = _dot(pooled.astype(jnp.bfloat16),
                 wgrp_ref[lo:lo + B_GROUP_DIM, :B_GROUP_DIM])
        act_ref[sub, :, cols] = (y * scale_ref[:, cols]).astype(jnp.bfloat16)

    def half(sub, h):
        for gi in halves[h]:
            pool_group(sub, gi)
        lo = min(halves[h]) * B_GROUP_DIM
        hi = (max(halves[h]) + 1) * B_GROUP_DIM
        part = _dot(act_ref[sub, :, lo:hi], wout_ref[lo:hi, :D_MODEL])
        rows = rows_of(sub)
        if h == 0:
            o_ref[rows, :] = x_ref[rows, :] + part
        else:
            o_ref[rows, :] += part

    in_proj(0)
    for sub in range(POOL_SUBTILES):
        half(sub, 0)
        if sub + 1 < POOL_SUBTILES:
            in_proj(sub + 1)
        half(sub, 1)


def _mixb(x2, seq, g, w_in, w_grp, scale, w_out, casts):
    t, d = x2.shape
    tb = POOL_SUBTILES * TM
    n_steps = t // tb
    c_in, c_out, c_shapes = _cast_specs(casts, n_steps)
    kern = functools.partial(_mixb_kernel, n_cast=len(casts), seq_steps=seq // tb)
    outs = pl.pallas_call(
        kern,
        grid=(n_steps,),
        in_specs=[
            pl.BlockSpec((tb, d), lambda i: (i, 0)),
            _resident((1, d)),
            _resident(w_in.shape),
            _resident(w_grp.shape),
            _resident((1, d)),
            _resident(w_out.shape),
        ] + c_in,
        out_specs=[pl.BlockSpec((tb, d), lambda i: (i, 0))] + c_out,
        out_shape=[jax.ShapeDtypeStruct((t, d), jnp.float32)] + c_shapes,
        scratch_shapes=[
            pltpu.VMEM((POOL_SUBTILES, HALO + TM, d), jnp.float32),
            pltpu.VMEM((HALO + TM, B_GROUP_DIM), jnp.float32),
            pltpu.VMEM((HALO + TM, B_GROUP_DIM), jnp.float32),
            pltpu.VMEM((POOL_SUBTILES, TM, d), jnp.bfloat16),
            pltpu.VMEM((POOL_SUBTILES, TM, B_GROUP_DIM), jnp.float32),
            pltpu.VMEM((POOL_SUBTILES, TM, d), jnp.bfloat16),
        ],
        compiler_params=pltpu.CompilerParams(
            dimension_semantics=("arbitrary",), vmem_limit_bytes=VMEM_LIMIT),
        name="mix_pool",
    )(x2, g, w_in, w_grp, scale, w_out, *[w for w, _ in casts])
    return outs[0], outs[1:]


def _stack3(w):
    return w.reshape((-1,) + w.shape[-2:])


def kernel(x, ffn_norm, ffn_w_in, ffn_w_out, mix_norm, a_w_in, a_v_norm, a_w_s,
           a_b_s, a_w_out, b_w_in, b_w_grp, b_scale, b_w_out, final_norm):
    bsz, seq, d = x.shape
    depth = ffn_norm.shape[0]
    row = lambda v: v.reshape(1, -1)
    fw_in, fw_out = _stack3(ffn_w_in), _stack3(ffn_w_out)
    bw_grp = b_w_grp.reshape(b_w_grp.shape[0], -1, b_w_grp.shape[-1])

    subs = []
    for i in range(depth):
        subs.append(("ffn", i, 0, [(fw_in, 2 * i), (fw_out, 2 * i)]))
        j = i // 2
        if i % 2 == 0:
            subs.append(("sgu", i, j, [(a_w_in, j), (a_w_out, j)]))
        else:
            subs.append(("pool", i, j, [(b_w_in, j), (bw_grp, j), (b_w_out, j)]))
        subs.append(("ffn", i, 1, [(fw_in, 2 * i + 1), (fw_out, 2 * i + 1)]))

    x2 = x.reshape(bsz * seq, d)
    fg = row(final_norm)
    weights = [_cast_padded(w[l]) for w, l in subs[0][3]]
    for n, (kind, i, k, _) in enumerate(subs):
        casts = subs[n + 1][3] if n + 1 < len(subs) else []
        if kind == "ffn":
            x2, nxt = _ffn(x2, row(ffn_norm[i, k]), weights[0], weights[1], fg,
                           casts, final_norm=(n == len(subs) - 1))
        elif kind == "sgu":
            x2, nxt = _mixa(x2, row(mix_norm[i]), weights[0], row(a_v_norm[k]),
                            a_w_s[k], a_b_s[k].T, weights[1], fg, casts)
        else:
            x2, nxt = _mixb(x2, seq, row(mix_norm[i]), weights[0], weights[1],
                            row(b_scale[k]), weights[2], casts)
        weights = list(nxt)
    return x2.reshape(bsz, seq, d)
```

```python
import functools

import jax
import jax.numpy as jnp
from jax import lax
from jax.experimental import pallas as pl
from jax.experimental.pallas import tpu as pltpu

D_MODEL = 1024
D_FF = 2816
CHUNK = 128
A_DIM = 2 * D_MODEL
A_HEADS = 8
A_HEAD_DIM = A_DIM // A_HEADS
POOL_WINDOWS = (2, 4, 8, 16)
B_GROUPS = len(POOL_WINDOWS)
B_GROUP_DIM = D_MODEL // B_GROUPS
EPS = 1e-6

HALO = 32
MXU_N = 256
LANES = 128
BF16_SUBLANES = 16
VMEM_LIMIT = 56 * 1024 * 1024

TM = 512


def _rmsnorm(x, g):
    ms = jnp.mean(x * x, axis=-1, keepdims=True)
    return x * lax.rsqrt(ms + EPS) * g


def _dot(a, b):
    return jnp.dot(a, b, preferred_element_type=jnp.float32)


def _resident(shape):
    return pl.BlockSpec(shape, lambda *_: (0,) * len(shape),
                        pipeline_mode=pl.Buffered(1))


def _cast_blocks(rows, n_steps):
    best = 1
    for nb in range(1, n_steps + 1):
        if rows % nb == 0 and (rows // nb) % BF16_SUBLANES == 0:
            best = nb
    return best


def _padded_cols(cols):
    return cols + LANES if (cols // LANES) % 8 == 0 else cols


def _cast_padded(w):
    pad = _padded_cols(w.shape[-1]) - w.shape[-1]
    wb = w.astype(jnp.bfloat16)
    if pad == 0:
        return wb
    return jnp.concatenate([wb, jnp.zeros((w.shape[0], pad), jnp.bfloat16)], axis=1)


def _cast_specs(casts, n_steps):
    in_specs, out_specs, out_shapes = [], [], []
    for w, layer in casts:
        _, rows, cols = w.shape
        nb = _cast_blocks(rows, n_steps)
        rb = rows // nb
        in_specs.append(pl.BlockSpec(
            (None, rb, cols),
            lambda i, layer=layer, nb=nb: (layer, jnp.minimum(i, nb - 1), 0)))
        out_specs.append(pl.BlockSpec(
            (rb, _padded_cols(cols)),
            lambda i, nb=nb: (jnp.minimum(i, nb - 1), 0)))
        out_shapes.append(
            jax.ShapeDtypeStruct((rows, _padded_cols(cols)), jnp.bfloat16))
    return in_specs, out_specs, out_shapes


def _run_casts(src_refs, dst_refs):
    for src, dst in zip(src_refs, dst_refs):
        rows, cols = src.shape
        dst[:, :cols] = src[...].astype(jnp.bfloat16)
        if dst.shape[1] > cols:
            dst[:, cols:] = jnp.zeros((rows, dst.shape[1] - cols), jnp.bfloat16)


FFN_SUBTILES = 2


def _ffn_kernel(*refs, n_cast, final_norm):
    x_ref, g_ref, win_ref, wout_ref, fg_ref = refs[:5]
    cast_src = refs[5:5 + n_cast]
    o_ref = refs[5 + n_cast]
    cast_dst = refs[6 + n_cast:6 + 2 * n_cast]
    xg_ref, a_ref, r_ref = refs[6 + 2 * n_cast:]
    _run_casts(cast_src, cast_dst)
    for r0 in range(0, x_ref.shape[0], TM):
        rows = pl.ds(r0, TM)
        x = x_ref[rows, :]
        xg_ref[rows, :] = (x * g_ref[...]).astype(jnp.bfloat16)
        r = lax.rsqrt(jnp.mean(x * x, axis=-1, keepdims=True) + EPS)
        r_ref[rows, :] = jnp.broadcast_to(r, (TM, MXU_N))
        for c in range(D_FF // MXU_N):
            lo = c * MXU_N
            gate = _dot(xg_ref[rows, :], win_ref[:, lo:lo + MXU_N]) * r_ref[rows, :]
            up = _dot(xg_ref[rows, :],
                      win_ref[:, D_FF + lo:D_FF + lo + MXU_N]) * r_ref[rows, :]
            a_ref[rows, lo:lo + MXU_N] = (
                gate * jax.nn.sigmoid(gate) * up).astype(jnp.bfloat16)
        y = x_ref[rows, :] + 0.5 * _dot(a_ref[rows, :], wout_ref[:, :D_MODEL])
        if final_norm:
            y = _rmsnorm(y, fg_ref[...])
        o_ref[rows, :] = y


def _ffn(x2, g, w_in, w_out, fg, casts, *, final_norm):
    t, d = x2.shape
    tb = FFN_SUBTILES * TM
    n_steps = t // tb
    c_in, c_out, c_shapes = _cast_specs(casts, n_steps)
    kern = functools.partial(_ffn_kernel, n_cast=len(casts), final_norm=final_norm)
    outs = pl.pallas_call(
        kern,
        grid=(n_steps,),
        in_specs=[
            pl.BlockSpec((tb, d), lambda i: (i, 0)),
            _resident((1, d)),
            _resident(w_in.shape),
            _resident(w_out.shape),
            _resident((1, d)),
        ] + c_in,
        out_specs=[pl.BlockSpec((tb, d), lambda i: (i, 0))] + c_out,
        out_shape=[jax.ShapeDtypeStruct((t, d), jnp.float32)] + c_shapes,
        scratch_shapes=[
            pltpu.VMEM((tb, d), jnp.bfloat16),
            pltpu.VMEM((tb, D_FF), jnp.bfloat16),
            pltpu.VMEM((tb, MXU_N), jnp.float32),
        ],
        compiler_params=pltpu.CompilerParams(
            dimension_semantics=("arbitrary",), vmem_limit_bytes=VMEM_LIMIT),
        name="ffn",
    )(x2, g, w_in, w_out, fg, *[w for w, _ in casts])
    return outs[0], outs[1:]


def _gelu_tanh(x):
    k = -2.0 * 0.7978845608028654 * 1.4426950408889634
    return x / (1.0 + jnp.exp2(x * (k + (k * 0.044715) * (x * x))))


SGU_SUBTILES = 2
SGU_OUT_BLOCKS = (2, 4)


def _mixa_kernel(*refs, n_cast):
    (x_ref, g_ref, win_ref, vn_ref, ws_ref, bst_ref, wout_ref) = refs[:7]
    cast_src = refs[7:7 + n_cast]
    o_ref = refs[7 + n_cast]
    cast_dst = refs[8 + n_cast:8 + 2 * n_cast]
    h_ref, u_ref, v_ref, act_ref = refs[8 + 2 * n_cast:]
    _run_casts(cast_src, cast_dst)
    n_chunks = 8
    col_chunk = A_DIM // n_chunks
    row_blocks, col_blocks = SGU_OUT_BLOCKS
    row = lax.broadcasted_iota(jnp.int32, (CHUNK, CHUNK), 0)
    col = lax.broadcasted_iota(jnp.int32, (CHUNK, CHUNK), 1)
    causal = col <= row

    def out_blocks(sub):
        def block(r, c):
            height, width = TM // row_blocks, D_MODEL // col_blocks
            rows = pl.ds(sub * TM + r * height, height)
            cols = slice(c * width, (c + 1) * width)
            o_ref[rows, cols] = x_ref[rows, cols] + _dot(
                act_ref[sub, r * height:(r + 1) * height, :], wout_ref[:, cols])
        return [functools.partial(block, r, c)
                for c in range(col_blocks) for r in range(row_blocks)]

    def activations(sub, fillers):
        rows = pl.ds(sub * TM, TM)
        fillers = list(fillers)
        n_after = -(-(len(fillers) - 1) // (n_chunks - 1)) if fillers else 0
        if fillers:
            fillers.pop(0)()
        h_ref[...] = _rmsnorm(x_ref[rows, :], g_ref[...]).astype(jnp.bfloat16)
        ssq = jnp.zeros((TM, 1), jnp.float32)
        for c in range(n_chunks):
            lo = c * col_chunk
            v = _gelu_tanh(
                _dot(h_ref[...], win_ref[:, A_DIM + lo:A_DIM + lo + col_chunk]))
            for _ in range(min(n_after, len(fillers))):
                fillers.pop(0)()
            ssq = ssq + jnp.sum(v * v, axis=-1, keepdims=True)
            v_ref[:, lo:lo + col_chunk] = v
        assert not fillers
        inv = lax.rsqrt(ssq * (1.0 / A_DIM) + EPS)

        def u_proj(hd):
            lo = hd * A_HEAD_DIM
            u_ref[:, lo:lo + A_HEAD_DIM] = _gelu_tanh(
                _dot(h_ref[...], win_ref[:, lo:lo + A_HEAD_DIM]))

        def gate_head(hd):
            lo = hd * A_HEAD_DIM
            ws = jnp.where(causal, ws_ref[hd], 0.0).astype(jnp.bfloat16)
            bias = bst_ref[:, hd:hd + 1]
            vn = (v_ref[:, lo:lo + A_HEAD_DIM] * inv
                  * vn_ref[:, lo:lo + A_HEAD_DIM]).astype(jnp.bfloat16)
            for ck in range(TM // CHUNK):
                r0 = ck * CHUNK
                gate = _dot(ws, vn[r0:r0 + CHUNK, :]) + bias
                act_ref[sub, r0:r0 + CHUNK, lo:lo + A_HEAD_DIM] = (
                    u_ref[r0:r0 + CHUNK, lo:lo + A_HEAD_DIM] * gate
                ).astype(jnp.bfloat16)

        u_proj(0)
        for hd in range(A_HEADS):
            if hd + 1 < A_HEADS:
                u_proj(hd + 1)
            gate_head(hd)

    pending = []
    for sub in range(SGU_SUBTILES):
        activations(sub, pending)
        pending = out_blocks(sub)
    for block in pending:
        block()


def _mixa(x2, g, w_in, v_norm, w_s, b_st, w_out, casts):
    t, d = x2.shape
    tb = SGU_SUBTILES * TM
    n_steps = t // tb
    c_in, c_out, c_shapes = _cast_specs(casts, n_steps)
    kern = functools.partial(_mixa_kernel, n_cast=len(casts))
    outs = pl.pallas_call(
        kern,
        grid=(n_steps,),
        in_specs=[
            pl.BlockSpec((tb, d), lambda i: (i, 0)),
            _resident((1, d)),
            _resident(w_in.shape),
            _resident((1, A_DIM)),
            _resident(w_s.shape),
            _resident(b_st.shape),
            _resident(w_out.shape),
        ] + c_in,
        out_specs=[pl.BlockSpec((tb, d), lambda i: (i, 0))] + c_out,
        out_shape=[jax.ShapeDtypeStruct((t, d), jnp.float32)] + c_shapes,
        scratch_shapes=[
            pltpu.VMEM((TM, d), jnp.bfloat16),
            pltpu.VMEM((TM, A_DIM), jnp.float32),
            pltpu.VMEM((TM, A_DIM), jnp.float32),
            pltpu.VMEM((SGU_SUBTILES, TM, A_DIM), jnp.bfloat16),
        ],
        compiler_params=pltpu.CompilerParams(
            dimension_semantics=("arbitrary",), vmem_limit_bytes=VMEM_LIMIT),
        name="mix_sgu",
    )(x2, g, w_in, v_norm, w_s, b_st, w_out, *[w for w, _ in casts])
    return outs[0], outs[1:]


POOL_SUBTILES = 2


def _mixb_kernel(*refs, n_cast, seq_steps):
    x_ref, g_ref, win_ref, wgrp_ref, scale_ref, wout_ref = refs[:6]
    cast_src = refs[6:6 + n_cast]
    o_ref = refs[6 + n_cast]
    cast_dst = refs[7 + n_cast:7 + 2 * n_cast]
    ext_ref, sum_a_ref, sum_b_ref, xg_ref, r_ref, act_ref = refs[7 + 2 * n_cast:]
    i = pl.program_id(0)
    rows_ext = HALO + TM
    last = POOL_SUBTILES - 1

    @pl.when(i == 0)
    def _():
        ext_ref[last, TM:rows_ext, :] = jnp.zeros((HALO, D_MODEL), jnp.float32)

    _run_casts(cast_src, cast_dst)
    step_in_seq = lax.rem(i, seq_steps)
    order = sorted(range(B_GROUPS), key=lambda gi: -POOL_WINDOWS[gi])
    halves = (order[:B_GROUPS // 2], order[B_GROUPS // 2:])

    def rows_of(sub):
        return pl.ds(sub * TM, TM)

    def in_proj(sub):
        x = x_ref[rows_of(sub), :]
        xg_ref[sub] = (x * g_ref[...]).astype(jnp.bfloat16)
        r = lax.rsqrt(jnp.mean(x * x, axis=-1, keepdims=True) + EPS)
        r_ref[sub] = jnp.broadcast_to(r, (TM, B_GROUP_DIM))
        carry = ext_ref[(sub - 1) % POOL_SUBTILES, TM:rows_ext, :]
        if sub == 0:
            carry = jnp.where(step_in_seq == 0, 0.0, carry)
        ext_ref[sub, 0:HALO, :] = carry
        for gi in order:
            cols = slice(gi * B_GROUP_DIM, (gi + 1) * B_GROUP_DIM)
            ext_ref[sub, HALO:, cols] = (
                _dot(xg_ref[sub], win_ref[:, cols]) * r_ref[sub])

    def pool_group(sub, gi):
        w = POOL_WINDOWS[gi]
        lo = gi * B_GROUP_DIM
        cols = slice(lo, lo + B_GROUP_DIM)
        pos = ((step_in_seq * POOL_SUBTILES + sub) * TM
               + lax.broadcasted_iota(jnp.int32, (TM, 1), 0))
        src, src_cols, level = ext_ref.at[sub], cols, 0
        k = 1
        while 2 * k < w:
            level += 1
            r0 = 8 * level
            dst = sum_a_ref if level % 2 else sum_b_ref
            dst[r0:rows_ext, :] = (src[r0:rows_ext, src_cols]
                                   + src[r0 - k:rows_ext - k, src_cols])
            src, src_cols = dst, slice(None)
            k *= 2
        acc = src[HALO:rows_ext, src_cols] + src[HALO - k:rows_ext - k, src_cols]
        cnt = jnp.minimum(pos + 1, w).astype(jnp.float32)
        pooled = acc / cnt - ext_ref[sub, HALO:, cols]
        y = _dot(pooled.astype(jnp.bfloat16),
                 wgrp_ref[lo:lo + B_GROUP_DIM, :B_GROUP_DIM])
        act_ref[sub, :, cols] = (y * scale_ref[:, cols]).astype(jnp.bfloat16)

    def half(sub, h):
        for gi in halves[h]:
            pool_group(sub, gi)
        lo = min(halves[h]) * B_GROUP_DIM
        hi = (max(halves[h]) + 1) * B_GROUP_DIM
        part = _dot(act_ref[sub, :, lo:hi], wout_ref[lo:hi, :D_MODEL])
        rows = rows_of(sub)
        if h == 0:
            o_ref[rows, :] = x_ref[rows, :] + part
        else:
            o_ref[rows, :] += part

    in_proj(0)
    for sub in range(POOL_SUBTILES):
        half(sub, 0)
        if sub + 1 < POOL_SUBTILES:
            in_proj(sub + 1)
        half(sub, 1)


def _mixb(x2, seq, g, w_in, w_grp, scale, w_out, casts):
    t, d = x2.shape
    tb = POOL_SUBTILES * TM
    n_steps = t // tb
    c_in, c_out, c_shapes = _cast_specs(casts, n_steps)
    kern = functools.partial(_mixb_kernel, n_cast=len(casts), seq_steps=seq // tb)
    outs = pl.pallas_call(
        kern,
        grid=(n_steps,),
        in_specs=[
            pl.BlockSpec((tb, d), lambda i: (i, 0)),
            _resident((1, d)),
            _resident(w_in.shape),
            _resident(w_grp.shape),
            _resident((1, d)),
            _resident(w_out.shape),
        ] + c_in,
        out_specs=[pl.BlockSpec((tb, d), lambda i: (i, 0))] + c_out,
        out_shape=[jax.ShapeDtypeStruct((t, d), jnp.float32)] + c_shapes,
        scratch_shapes=[
            pltpu.VMEM((POOL_SUBTILES, HALO + TM, d), jnp.float32),
            pltpu.VMEM((HALO + TM, B_GROUP_DIM), jnp.float32),
            pltpu.VMEM((HALO + TM, B_GROUP_DIM), jnp.float32),
            pltpu.VMEM((POOL_SUBTILES, TM, d), jnp.bfloat16),
            pltpu.VMEM((POOL_SUBTILES, TM, B_GROUP_DIM), jnp.float32),
            pltpu.VMEM((POOL_SUBTILES, TM, d), jnp.bfloat16),
        ],
        compiler_params=pltpu.CompilerParams(
            dimension_semantics=("arbitrary",), vmem_limit_bytes=VMEM_LIMIT),
        name="mix_pool",
    )(x2, g, w_in, w_grp, scale, w_out, *[w for w, _ in casts])
    return outs[0], outs[1:]


def _stack3(w):
    return w.reshape((-1,) + w.shape[-2:])


def kernel(x, ffn_norm, ffn_w_in, ffn_w_out, mix_norm, a_w_in, a_v_norm, a_w_s,
           a_b_s, a_w_out, b_w_in, b_w_grp, b_scale, b_w_out, final_norm):
    bsz, seq, d = x.shape
    depth = ffn_norm.shape[0]
    row = lambda v: v.reshape(1, -1)
    fw_in, fw_out = _stack3(ffn_w_in), _stack3(ffn_w_out)
    bw_grp = b_w_grp.reshape(b_w_grp.shape[0], -1, b_w_grp.shape[-1])

    subs = []
    for i in range(depth):
        subs.append(("ffn", i, 0, [(fw_in, 2 * i), (fw_out, 2 * i)]))
        j = i // 2
        if i % 2 == 0:
            subs.append(("sgu", i, j, [(a_w_in, j), (a_w_out, j)]))
        else:
            subs.append(("pool", i, j, [(b_w_in, j), (bw_grp, j), (b_w_out, j)]))
        subs.append(("ffn", i, 1, [(fw_in, 2 * i + 1), (fw_out, 2 * i + 1)]))

    x2 = x.reshape(bsz * seq, d)
    fg = row(final_norm)
    weights = [_cast_padded(w[l]) for w, l in subs[0][3]]
    for n, (kind, i, k, _) in enumerate(subs):
        casts = subs[n + 1][3] if n + 1 < len(subs) else []
        if kind == "ffn":
            x2, nxt = _ffn(x2, row(ffn_norm[i, k]), weights[0], weights[1], fg,
                           casts, final_norm=(n == len(subs) - 1))
        elif kind == "sgu":
            x2, nxt = _mixa(x2, row(mix_norm[i]), weights[0], row(a_v_norm[k]),
                            a_w_s[k], a_b_s[k].T, weights[1], casts)
        else:
            x2, nxt = _mixb(x2, seq, row(mix_norm[i]), weights[0], weights[1],
                            row(b_scale[k]), weights[2], casts)
        weights = list(nxt)
    return x2.reshape(bsz, seq, d)
```

```python
import functools

import jax
import jax.numpy as jnp
from jax import lax
from jax.experimental import pallas as pl
from jax.experimental.pallas import tpu as pltpu

D_MODEL = 1024
D_FF = 2816
CHUNK = 128
A_DIM = 2 * D_MODEL
A_HEADS = 8
A_HEAD_DIM = A_DIM // A_HEADS
POOL_WINDOWS = (2, 4, 8, 16)
B_GROUPS = len(POOL_WINDOWS)
B_GROUP_DIM = D_MODEL // B_GROUPS
EPS = 1e-6

HALO = 32
MXU_N = 256
LANES = 128
BF16_SUBLANES = 16
VMEM_LIMIT = 56 * 1024 * 1024

TM = 512


def _rmsnorm(x, g):
    ms = jnp.mean(x * x, axis=-1, keepdims=True)
    return x * lax.rsqrt(ms + EPS) * g


def _dot(a, b):
    return jnp.dot(a, b, preferred_element_type=jnp.float32)


def _resident(shape):
    return pl.BlockSpec(shape, lambda *_: (0,) * len(shape),
                        pipeline_mode=pl.Buffered(1))


def _cast_blocks(rows, n_steps):
    best = 1
    for nb in range(1, n_steps + 1):
        if rows % nb == 0 and (rows // nb) % BF16_SUBLANES == 0:
            best = nb
    return best


def _padded_cols(cols):
    return cols + LANES if (cols // LANES) % 8 == 0 else cols


def _cast_padded(w):
    pad = _padded_cols(w.shape[-1]) - w.shape[-1]
    wb = w.astype(jnp.bfloat16)
    if pad == 0:
        return wb
    return jnp.concatenate([wb, jnp.zeros((w.shape[0], pad), jnp.bfloat16)], axis=1)


def _cast_specs(casts, n_steps):
    in_specs, out_specs, out_shapes = [], [], []
    for w, layer in casts:
        _, rows, cols = w.shape
        nb = _cast_blocks(rows, n_steps)
        rb = rows // nb
        in_specs.append(pl.BlockSpec(
            (None, rb, cols),
            lambda i, layer=layer, nb=nb: (layer, jnp.minimum(i, nb - 1), 0)))
        out_specs.append(pl.BlockSpec(
            (rb, _padded_cols(cols)),
            lambda i, nb=nb: (jnp.minimum(i, nb - 1), 0)))
        out_shapes.append(
            jax.ShapeDtypeStruct((rows, _padded_cols(cols)), jnp.bfloat16))
    return in_specs, out_specs, out_shapes


def _run_casts(src_refs, dst_refs):
    for src, dst in zip(src_refs, dst_refs):
        rows, cols = src.shape
        dst[:, :cols] = src[...].astype(jnp.bfloat16)
        if dst.shape[1] > cols:
            dst[:, cols:] = jnp.zeros((rows, dst.shape[1] - cols), jnp.bfloat16)


FFN_SUBTILES = 4
FFN_SLOTS = 2
FFN_VMEM_LIMIT = 62 * 1024 * 1024


def _ffn_kernel(*refs, n_cast, final_norm):
    x_ref, g_ref, win_ref, wout_ref, fg_ref = refs[:5]
    cast_src = refs[5:5 + n_cast]
    o_ref = refs[5 + n_cast]
    cast_dst = refs[6 + n_cast:6 + 2 * n_cast]
    xg_ref, a_ref, r_ref = refs[6 + 2 * n_cast:]
    _run_casts(cast_src, cast_dst)
    for sub in range(x_ref.shape[0] // TM):
        rows = pl.ds(sub * TM, TM)
        slot = sub % FFN_SLOTS
        x = x_ref[rows, :]
        xg_ref[slot] = (x * g_ref[...]).astype(jnp.bfloat16)
        r = lax.rsqrt(jnp.mean(x * x, axis=-1, keepdims=True) + EPS)
        r_ref[slot] = jnp.broadcast_to(r, (TM, MXU_N))
        for c in range(D_FF // MXU_N):
            lo = c * MXU_N
            gate = _dot(xg_ref[slot], win_ref[:, lo:lo + MXU_N]) * r_ref[slot]
            up = _dot(xg_ref[slot],
                      win_ref[:, D_FF + lo:D_FF + lo + MXU_N]) * r_ref[slot]
            a_ref[slot, :, lo:lo + MXU_N] = (
                gate * jax.nn.sigmoid(gate) * up).astype(jnp.bfloat16)
        y = x_ref[rows, :] + 0.5 * _dot(a_ref[slot], wout_ref[:, :D_MODEL])
        if final_norm:
            y = _rmsnorm(y, fg_ref[...])
        o_ref[rows, :] = y


def _ffn(x2, g, w_in, w_out, fg, casts, *, final_norm):
    t, d = x2.shape
    tb = FFN_SUBTILES * TM
    n_steps = t // tb
    c_in, c_out, c_shapes = _cast_specs(casts, n_steps)
    kern = functools.partial(_ffn_kernel, n_cast=len(casts), final_norm=final_norm)
    outs = pl.pallas_call(
        kern,
        grid=(n_steps,),
        in_specs=[
            pl.BlockSpec((tb, d), lambda i: (i, 0)),
            _resident((1, d)),
            _resident(w_in.shape),
            _resident(w_out.shape),
            _resident((1, d)),
        ] + c_in,
        out_specs=[pl.BlockSpec((tb, d), lambda i: (i, 0))] + c_out,
        out_shape=[jax.ShapeDtypeStruct((t, d), jnp.float32)] + c_shapes,
        scratch_shapes=[
            pltpu.VMEM((FFN_SLOTS, TM, d), jnp.bfloat16),
            pltpu.VMEM((FFN_SLOTS, TM, D_FF), jnp.bfloat16),
            pltpu.VMEM((FFN_SLOTS, TM, MXU_N), jnp.float32),
        ],
        compiler_params=pltpu.CompilerParams(
            dimension_semantics=("arbitrary",), vmem_limit_bytes=FFN_VMEM_LIMIT),
        name="ffn",
    )(x2, g, w_in, w_out, fg, *[w for w, _ in casts])
    return outs[0], outs[1:]


def _gelu_tanh(x):
    k = -2.0 * 0.7978845608028654 * 1.4426950408889634
    return x / (1.0 + jnp.exp2(x * (k + (k * 0.044715) * (x * x))))


SGU_SUBTILES = 2
SGU_OUT_BLOCKS = (2, 4)


def _mixa_kernel(*refs, n_cast):
    (x_ref, g_ref, win_ref, vn_ref, ws_ref, bst_ref, wout_ref) = refs[:7]
    cast_src = refs[7:7 + n_cast]
    o_ref = refs[7 + n_cast]
    cast_dst = refs[8 + n_cast:8 + 2 * n_cast]
    h_ref, u_ref, v_ref, act_ref = refs[8 + 2 * n_cast:]
    _run_casts(cast_src, cast_dst)
    n_chunks = 8
    col_chunk = A_DIM // n_chunks
    row_blocks, col_blocks = SGU_OUT_BLOCKS
    row = lax.broadcasted_iota(jnp.int32, (CHUNK, CHUNK), 0)
    col = lax.broadcasted_iota(jnp.int32, (CHUNK, CHUNK), 1)
    causal = col <= row

    def out_blocks(sub):
        def block(r, c):
            height, width = TM // row_blocks, D_MODEL // col_blocks
            rows = pl.ds(sub * TM + r * height, height)
            cols = slice(c * width, (c + 1) * width)
            o_ref[rows, cols] = x_ref[rows, cols] + _dot(
                act_ref[sub, r * height:(r + 1) * height, :], wout_ref[:, cols])
        return [functools.partial(block, r, c)
                for c in range(col_blocks) for r in range(row_blocks)]

    def activations(sub, fillers):
        rows = pl.ds(sub * TM, TM)
        fillers = list(fillers)
        n_after = -(-(len(fillers) - 1) // (n_chunks - 1)) if fillers else 0
        if fillers:
            fillers.pop(0)()
        h_ref[...] = _rmsnorm(x_ref[rows, :], g_ref[...]).astype(jnp.bfloat16)
        ssq = jnp.zeros((TM, 1), jnp.float32)
        for c in range(n_chunks):
            lo = c * col_chunk
            v = _gelu_tanh(
                _dot(h_ref[...], win_ref[:, A_DIM + lo:A_DIM + lo + col_chunk]))
            for _ in range(min(n_after, len(fillers))):
                fillers.pop(0)()
            ssq = ssq + jnp.sum(v * v, axis=-1, keepdims=True)
            v_ref[:, lo:lo + col_chunk] = v
        assert not fillers
        inv = lax.rsqrt(ssq * (1.0 / A_DIM) + EPS)

        def u_proj(hd):
            lo = hd * A_HEAD_DIM
            u_ref[:, lo:lo + A_HEAD_DIM] = _gelu_tanh(
                _dot(h_ref[...], win_ref[:, lo:lo + A_HEAD_DIM]))

        def gate_head(hd):
            lo = hd * A_HEAD_DIM
            ws = jnp.where(causal, ws_ref[hd], 0.0).astype(jnp.bfloat16)
            bias = bst_ref[:, hd:hd + 1]
            vn = (v_ref[:, lo:lo + A_HEAD_DIM] * inv
                  * vn_ref[:, lo:lo + A_HEAD_DIM]).astype(jnp.bfloat16)
            for ck in range(TM // CHUNK):
                r0 = ck * CHUNK
                gate = _dot(ws, vn[r0:r0 + CHUNK, :]) + bias
                act_ref[sub, r0:r0 + CHUNK, lo:lo + A_HEAD_DIM] = (
                    u_ref[r0:r0 + CHUNK, lo:lo + A_HEAD_DIM] * gate
                ).astype(jnp.bfloat16)

        u_proj(0)
        for hd in range(A_HEADS):
            if hd + 1 < A_HEADS:
                u_proj(hd + 1)
            gate_head(hd)

    pending = []
    for sub in range(SGU_SUBTILES):
        activations(sub, pending)
        pending = out_blocks(sub)
    for block in pending:
        block()


def _mixa(x2, g, w_in, v_norm, w_s, b_st, w_out, casts):
    t, d = x2.shape
    tb = SGU_SUBTILES * TM
    n_steps = t // tb
    c_in, c_out, c_shapes = _cast_specs(casts, n_steps)
    kern = functools.partial(_mixa_kernel, n_cast=len(casts))
    outs = pl.pallas_call(
        kern,
        grid=(n_steps,),
        in_specs=[
            pl.BlockSpec((tb, d), lambda i: (i, 0)),
            _resident((1, d)),
            _resident(w_in.shape),
            _resident((1, A_DIM)),
            _resident(w_s.shape),
            _resident(b_st.shape),
            _resident(w_out.shape),
        ] + c_in,
        out_specs=[pl.BlockSpec((tb, d), lambda i: (i, 0))] + c_out,
        out_shape=[jax.ShapeDtypeStruct((t, d), jnp.float32)] + c_shapes,
        scratch_shapes=[
            pltpu.VMEM((TM, d), jnp.bfloat16),
            pltpu.VMEM((TM, A_DIM), jnp.float32),
            pltpu.VMEM((TM, A_DIM), jnp.float32),
            pltpu.VMEM((SGU_SUBTILES, TM, A_DIM), jnp.bfloat16),
        ],
        compiler_params=pltpu.CompilerParams(
            dimension_semantics=("arbitrary",), vmem_limit_bytes=VMEM_LIMIT),
        name="mix_sgu",
    )(x2, g, w_in, v_norm, w_s, b_st, w_out, *[w for w, _ in casts])
    return outs[0], outs[1:]


POOL_SUBTILES = 2


def _mixb_kernel(*refs, n_cast, seq_steps):
    x_ref, g_ref, win_ref, wgrp_ref, scale_ref, wout_ref = refs[:6]
    cast_src = refs[6:6 + n_cast]
    o_ref = refs[6 + n_cast]
    cast_dst = refs[7 + n_cast:7 + 2 * n_cast]
    ext_ref, sum_a_ref, sum_b_ref, xg_ref, r_ref, act_ref = refs[7 + 2 * n_cast:]
    i = pl.program_id(0)
    rows_ext = HALO + TM
    last = POOL_SUBTILES - 1

    @pl.when(i == 0)
    def _():
        ext_ref[last, TM:rows_ext, :] = jnp.zeros((HALO, D_MODEL), jnp.float32)

    _run_casts(cast_src, cast_dst)
    step_in_seq = lax.rem(i, seq_steps)
    order = sorted(range(B_GROUPS), key=lambda gi: -POOL_WINDOWS[gi])
    halves = (order[:B_GROUPS // 2], order[B_GROUPS // 2:])

    def rows_of(sub):
        return pl.ds(sub * TM, TM)

    def in_proj(sub):
        x = x_ref[rows_of(sub), :]
        xg_ref[sub] = (x * g_ref[...]).astype(jnp.bfloat16)
        r = lax.rsqrt(jnp.mean(x * x, axis=-1, keepdims=True) + EPS)
        r_ref[sub] = jnp.broadcast_to(r, (TM, B_GROUP_DIM))
        carry = ext_ref[(sub - 1) % POOL_SUBTILES, TM:rows_ext, :]
        if sub == 0:
            carry = jnp.where(step_in_seq == 0, 0.0, carry)
        ext_ref[sub, 0:HALO, :] = carry
        for gi in order:
            cols = slice(gi * B_GROUP_DIM, (gi + 1) * B_GROUP_DIM)
            ext_ref[sub, HALO:, cols] = (
                _dot(xg_ref[sub], win_ref[:, cols]) * r_ref[sub])

    def pool_group(sub, gi):
        w = POOL_WINDOWS[gi]
        lo = gi * B_GROUP_DIM
        cols = slice(lo, lo + B_GROUP_DIM)
        pos = ((step_in_seq * POOL_SUBTILES + sub) * TM
               + lax.broadcasted_iota(jnp.int32, (TM, 1), 0))
        src, src_cols, level = ext_ref.at[sub], cols, 0
        k = 1
        while 2 * k < w:
            level += 1
            r0 = 8 * level
            dst = sum_a_ref if level % 2 else sum_b_ref
            dst[r0:rows_ext, :] = (src[r0:rows_ext, src_cols]
                                   + src[r0 - k:rows_ext - k, src_cols])
            src, src_cols = dst, slice(None)
            k *= 2
        acc = src[HALO:rows_ext, src_cols] + src[HALO - k:rows_ext - k, src_cols]
        cnt = jnp.minimum(pos + 1, w).astype(jnp.float32)
        pooled = acc / cnt - ext_ref[sub, HALO:, cols]
        y = _dot(pooled.astype(jnp.bfloat16),
                 wgrp_ref[lo:lo + B_GROUP_DIM, :B_GROUP_DIM])
        act_ref[sub, :, cols] = (y * scale_ref[:, cols]).astype(jnp.bfloat16)

    def half(sub, h):
        for gi in halves[h]:
            pool_group(sub, gi)
        lo = min(halves[h]) * B_GROUP_DIM
        hi = (max(halves[h]) + 1) * B_GROUP_DIM
        part = _dot(act_ref[sub, :, lo:hi], wout_ref[lo:hi, :D_MODEL])
        rows = rows_of(sub)
        if h == 0:
            o_ref[rows, :] = x_ref[rows, :] + part
        else:
            o_ref[rows, :] += part

    in_proj(0)
    for sub in range(POOL_SUBTILES):
        half(sub, 0)
        if sub + 1 < POOL_SUBTILES:
            in_proj(sub + 1)
        half(sub, 1)


def _mixb(x2, seq, g, w_in, w_grp, scale, w_out, casts):
    t, d = x2.shape
    tb = POOL_SUBTILES * TM
    n_steps = t // tb
    c_in, c_out, c_shapes = _cast_specs(casts, n_steps)
    kern = functools.partial(_mixb_kernel, n_cast=len(casts), seq_steps=seq // tb)
    outs = pl.pallas_call(
        kern,
        grid=(n_steps,),
        in_specs=[
            pl.BlockSpec((tb, d), lambda i: (i, 0)),
            _resident((1, d)),
            _resident(w_in.shape),
            _resident(w_grp.shape),
            _resident((1, d)),
            _resident(w_out.shape),
        ] + c_in,
        out_specs=[pl.BlockSpec((tb, d), lambda i: (i, 0))] + c_out,
        out_shape=[jax.ShapeDtypeStruct((t, d), jnp.float32)] + c_shapes,
        scratch_shapes=[
            pltpu.VMEM((POOL_SUBTILES, HALO + TM, d), jnp.float32),
            pltpu.VMEM((HALO + TM, B_GROUP_DIM), jnp.float32),
            pltpu.VMEM((HALO + TM, B_GROUP_DIM), jnp.float32),
            pltpu.VMEM((POOL_SUBTILES, TM, d), jnp.bfloat16),
            pltpu.VMEM((POOL_SUBTILES, TM, B_GROUP_DIM), jnp.float32),
            pltpu.VMEM((POOL_SUBTILES, TM, d), jnp.bfloat16),
        ],
        compiler_params=pltpu.CompilerParams(
            dimension_semantics=("arbitrary",), vmem_limit_bytes=VMEM_LIMIT),
        name="mix_pool",
    )(x2, g, w_in, w_grp, scale, w_out, *[w for w, _ in casts])
    return outs[0], outs[1:]


def _stack3(w):
    return w.reshape((-1,) + w.shape[-2:])


def kernel(x, ffn_norm, ffn_w_in, ffn_w_out, mix_norm, a_w_in, a_v_norm, a_w_s,
           a_b_s, a_w_out, b_w_in, b_w_grp, b_scale, b_w_out, final_norm):
    bsz, seq, d = x.shape
    depth = ffn_norm.shape[0]
    row = lambda v: v.reshape(1, -1)
    fw_in, fw_out = _stack3(ffn_w_in), _stack3(ffn_w_out)
    bw_grp = b_w_grp.reshape(b_w_grp.shape[0], -1, b_w_grp.shape[-1])

    subs = []
    for i in range(depth):
        subs.append(("ffn", i, 0, [(fw_in, 2 * i), (fw_out, 2 * i)]))
        j = i // 2
        if i % 2 == 0:
            subs.append(("sgu", i, j, [(a_w_in, j), (a_w_out, j)]))
        else:
            subs.append(("pool", i, j, [(b_w_in, j), (bw_grp, j), (b_w_out, j)]))
        subs.append(("ffn", i, 1, [(fw_in, 2 * i + 1), (fw_out, 2 * i + 1)]))

    x2 = x.reshape(bsz * seq, d)
    fg = row(final_norm)
    weights = [_cast_padded(w[l]) for w, l in subs[0][3]]
    for n, (kind, i, k, _) in enumerate(subs):
        casts = subs[n + 1][3] if n + 1 < len(subs) else []
        if kind == "ffn":
            x2, nxt = _ffn(x2, row(ffn_norm[i, k]), weights[0], weights[1], fg,
                           casts, final_norm=(n == len(subs) - 1))
        elif kind == "sgu":
            x2, nxt = _mixa(x2, row(mix_norm[i]), weights[0], row(a_v_norm[k]),
                            a_w_s[k], a_b_s[k].T, weights[1], casts)
        else:
            x2, nxt = _mixb(x2, seq, row(mix_norm[i]), weights[0], weights[1],
                            row(b_scale[k]), weights[2], casts)
        weights = list(nxt)
    return x2.reshape(bsz, seq, d)
```

```python
import functools

import jax
import jax.numpy as jnp
from jax import lax
from jax.experimental import pallas as pl
from jax.experimental.pallas import tpu as pltpu

D_MODEL = 1024
D_FF = 2816
CHUNK = 128
A_DIM = 2 * D_MODEL
A_HEADS = 8
A_HEAD_DIM = A_DIM // A_HEADS
POOL_WINDOWS = (2, 4, 8, 16)
B_GROUPS = len(POOL_WINDOWS)
B_GROUP_DIM = D_MODEL // B_GROUPS
EPS = 1e-6

HALO = 32
MXU_N = 256
LANES = 128
BF16_SUBLANES = 16
VMEM_LIMIT = 56 * 1024 * 1024

TM = 512


def _rmsnorm(x, g):
    ms = jnp.mean(x * x, axis=-1, keepdims=True)
    return x * lax.rsqrt(ms + EPS) * g


def _dot(a, b):
    return jnp.dot(a, b, preferred_element_type=jnp.float32)


def _resident(shape):
    return pl.BlockSpec(shape, lambda *_: (0,) * len(shape),
                        pipeline_mode=pl.Buffered(1))


def _cast_blocks(rows, n_steps):
    best = 1
    for nb in range(1, n_steps + 1):
        if rows % nb == 0 and (rows // nb) % BF16_SUBLANES == 0:
            best = nb
    return best


def _padded_cols(cols):
    return cols + LANES if (cols // LANES) % 8 == 0 else cols


def _cast_padded(w):
    pad = _padded_cols(w.shape[-1]) - w.shape[-1]
    wb = w.astype(jnp.bfloat16)
    if pad == 0:
        return wb
    return jnp.concatenate([wb, jnp.zeros((w.shape[0], pad), jnp.bfloat16)], axis=1)


def _cast_specs(casts, n_steps):
    in_specs, out_specs, out_shapes = [], [], []
    for w, layer in casts:
        _, rows, cols = w.shape
        nb = _cast_blocks(rows, n_steps)
        rb = rows // nb
        in_specs.append(pl.BlockSpec(
            (None, rb, cols),
            lambda i, layer=layer, nb=nb: (layer, jnp.minimum(i, nb - 1), 0)))
        out_specs.append(pl.BlockSpec(
            (rb, _padded_cols(cols)),
            lambda i, nb=nb: (jnp.minimum(i, nb - 1), 0)))
        out_shapes.append(
            jax.ShapeDtypeStruct((rows, _padded_cols(cols)), jnp.bfloat16))
    return in_specs, out_specs, out_shapes


def _run_casts(src_refs, dst_refs):
    for src, dst in zip(src_refs, dst_refs):
        rows, cols = src.shape
        dst[:, :cols] = src[...].astype(jnp.bfloat16)
        if dst.shape[1] > cols:
            dst[:, cols:] = jnp.zeros((rows, dst.shape[1] - cols), jnp.bfloat16)


FFN_SUBTILES = 2


def _ffn_kernel(*refs, n_cast, final_norm):
    x_ref, g_ref, win_ref, wout_ref, fg_ref = refs[:5]
    cast_src = refs[5:5 + n_cast]
    o_ref = refs[5 + n_cast]
    cast_dst = refs[6 + n_cast:6 + 2 * n_cast]
    xg_ref, a_ref, r_ref = refs[6 + 2 * n_cast:]
    _run_casts(cast_src, cast_dst)
    for r0 in range(0, x_ref.shape[0], TM):
        rows = pl.ds(r0, TM)
        x = x_ref[rows, :]
        xg_ref[rows, :] = (x * g_ref[...]).astype(jnp.bfloat16)
        r = lax.rsqrt(jnp.mean(x * x, axis=-1, keepdims=True) + EPS)
        r_ref[rows, :] = jnp.broadcast_to(r, (TM, MXU_N))
        for c in range(D_FF // MXU_N):
            lo = c * MXU_N
            gate = _dot(xg_ref[rows, :], win_ref[:, lo:lo + MXU_N]) * r_ref[rows, :]
            up_raw = _dot(xg_ref[rows, :], win_ref[:, D_FF + lo:D_FF + lo + MXU_N])
            a_ref[rows, lo:lo + MXU_N] = (
                gate * jax.nn.sigmoid(gate) * up_raw).astype(jnp.bfloat16)
        half_r = jnp.concatenate(
            [0.5 * r_ref[rows, :]] * (D_MODEL // MXU_N), axis=1)
        y = x_ref[rows, :] + half_r * _dot(a_ref[rows, :], wout_ref[:, :D_MODEL])
        if final_norm:
            y = _rmsnorm(y, fg_ref[...])
        o_ref[rows, :] = y


def _ffn(x2, g, w_in, w_out, fg, casts, *, final_norm):
    t, d = x2.shape
    tb = FFN_SUBTILES * TM
    n_steps = t // tb
    c_in, c_out, c_shapes = _cast_specs(casts, n_steps)
    kern = functools.partial(_ffn_kernel, n_cast=len(casts), final_norm=final_norm)
    outs = pl.pallas_call(
        kern,
        grid=(n_steps,),
        in_specs=[
            pl.BlockSpec((tb, d), lambda i: (i, 0)),
            _resident((1, d)),
            _resident(w_in.shape),
            _resident(w_out.shape),
            _resident((1, d)),
        ] + c_in,
        out_specs=[pl.BlockSpec((tb, d), lambda i: (i, 0))] + c_out,
        out_shape=[jax.ShapeDtypeStruct((t, d), jnp.float32)] + c_shapes,
        scratch_shapes=[
            pltpu.VMEM((tb, d), jnp.bfloat16),
            pltpu.VMEM((tb, D_FF), jnp.bfloat16),
            pltpu.VMEM((tb, MXU_N), jnp.float32),
        ],
        compiler_params=pltpu.CompilerParams(
            dimension_semantics=("arbitrary",), vmem_limit_bytes=VMEM_LIMIT),
        name="ffn",
    )(x2, g, w_in, w_out, fg, *[w for w, _ in casts])
    return outs[0], outs[1:]


def _gelu_tanh(x):
    k = -2.0 * 0.7978845608028654 * 1.4426950408889634
    return x / (1.0 + jnp.exp2(x * (k + (k * 0.044715) * (x * x))))


SGU_SUBTILES = 2
SGU_OUT_BLOCKS = (2, 4)


def _mixa_kernel(*refs, n_cast):
    (x_ref, g_ref, win_ref, vn_ref, ws_ref, bst_ref, wout_ref) = refs[:7]
    cast_src = refs[7:7 + n_cast]
    o_ref = refs[7 + n_cast]
    cast_dst = refs[8 + n_cast:8 + 2 * n_cast]
    h_ref, u_ref, v_ref, act_ref = refs[8 + 2 * n_cast:]
    _run_casts(cast_src, cast_dst)
    n_chunks = 8
    col_chunk = A_DIM // n_chunks
    row_blocks, col_blocks = SGU_OUT_BLOCKS
    row = lax.broadcasted_iota(jnp.int32, (CHUNK, CHUNK), 0)
    col = lax.broadcasted_iota(jnp.int32, (CHUNK, CHUNK), 1)
    causal = col <= row

    def out_blocks(sub):
        def block(r, c):
            height, width = TM // row_blocks, D_MODEL // col_blocks
            rows = pl.ds(sub * TM + r * height, height)
            cols = slice(c * width, (c + 1) * width)
            o_ref[rows, cols] = x_ref[rows, cols] + _dot(
                act_ref[sub, r * height:(r + 1) * height, :], wout_ref[:, cols])
        return [functools.partial(block, r, c)
                for c in range(col_blocks) for r in range(row_blocks)]

    def activations(sub, fillers):
        rows = pl.ds(sub * TM, TM)
        fillers = list(fillers)
        n_after = -(-(len(fillers) - 1) // (n_chunks - 1)) if fillers else 0
        if fillers:
            fillers.pop(0)()
        h_ref[...] = _rmsnorm(x_ref[rows, :], g_ref[...]).astype(jnp.bfloat16)
        ssq = jnp.zeros((TM, 1), jnp.float32)
        for c in range(n_chunks):
            lo = c * col_chunk
            v = _gelu_tanh(
                _dot(h_ref[...], win_ref[:, A_DIM + lo:A_DIM + lo + col_chunk]))
            for _ in range(min(n_after, len(fillers))):
                fillers.pop(0)()
            ssq = ssq + jnp.sum(v * v, axis=-1, keepdims=True)
            v_ref[:, lo:lo + col_chunk] = v
        assert not fillers
        inv = lax.rsqrt(ssq * (1.0 / A_DIM) + EPS)

        def u_proj(hd):
            lo = hd * A_HEAD_DIM
            u_ref[:, lo:lo + A_HEAD_DIM] = _gelu_tanh(
                _dot(h_ref[...], win_ref[:, lo:lo + A_HEAD_DIM]))

        def gate_head(hd):
            lo = hd * A_HEAD_DIM
            ws = jnp.where(causal, ws_ref[hd], 0.0).astype(jnp.bfloat16)
            bias = bst_ref[:, hd:hd + 1]
            vn = (v_ref[:, lo:lo + A_HEAD_DIM] * inv
                  * vn_ref[:, lo:lo + A_HEAD_DIM]).astype(jnp.bfloat16)
            for ck in range(TM // CHUNK):
                r0 = ck * CHUNK
                gate = _dot(ws, vn[r0:r0 + CHUNK, :]) + bias
                act_ref[sub, r0:r0 + CHUNK, lo:lo + A_HEAD_DIM] = (
                    u_ref[r0:r0 + CHUNK, lo:lo + A_HEAD_DIM] * gate
                ).astype(jnp.bfloat16)

        u_proj(0)
        for hd in range(A_HEADS):
            if hd + 1 < A_HEADS:
                u_proj(hd + 1)
            gate_head(hd)

    pending = []
    for sub in range(SGU_SUBTILES):
        activations(sub, pending)
        pending = out_blocks(sub)
    for block in pending:
        block()


def _mixa(x2, g, w_in, v_norm, w_s, b_st, w_out, casts):
    t, d = x2.shape
    tb = SGU_SUBTILES * TM
    n_steps = t // tb
    c_in, c_out, c_shapes = _cast_specs(casts, n_steps)
    kern = functools.partial(_mixa_kernel, n_cast=len(casts))
    outs = pl.pallas_call(
        kern,
        grid=(n_steps,),
        in_specs=[
            pl.BlockSpec((tb, d), lambda i: (i, 0)),
            _resident((1, d)),
            _resident(w_in.shape),
            _resident((1, A_DIM)),
            _resident(w_s.shape),
            _resident(b_st.shape),
            _resident(w_out.shape),
        ] + c_in,
        out_specs=[pl.BlockSpec((tb, d), lambda i: (i, 0))] + c_out,
        out_shape=[jax.ShapeDtypeStruct((t, d), jnp.float32)] + c_shapes,
        scratch_shapes=[
            pltpu.VMEM((TM, d), jnp.bfloat16),
            pltpu.VMEM((TM, A_DIM), jnp.float32),
            pltpu.VMEM((TM, A_DIM), jnp.float32),
            pltpu.VMEM((SGU_SUBTILES, TM, A_DIM), jnp.bfloat16),
        ],
        compiler_params=pltpu.CompilerParams(
            dimension_semantics=("arbitrary",), vmem_limit_bytes=VMEM_LIMIT),
        name="mix_sgu",
    )(x2, g, w_in, v_norm, w_s, b_st, w_out, *[w for w, _ in casts])
    return outs[0], outs[1:]


POOL_SUBTILES = 2


def _mixb_kernel(*refs, n_cast, seq_steps):
    x_ref, g_ref, win_ref, wgrp_ref, scale_ref, wout_ref = refs[:6]
    cast_src = refs[6:6 + n_cast]
    o_ref = refs[6 + n_cast]
    cast_dst = refs[7 + n_cast:7 + 2 * n_cast]
    ext_ref, sum_a_ref, sum_b_ref, xg_ref, r_ref, act_ref = refs[7 + 2 * n_cast:]
    i = pl.program_id(0)
    rows_ext = HALO + TM
    last = POOL_SUBTILES - 1

    @pl.when(i == 0)
    def _():
        ext_ref[last, TM:rows_ext, :] = jnp.zeros((HALO, D_MODEL), jnp.float32)

    _run_casts(cast_src, cast_dst)
    step_in_seq = lax.rem(i, seq_steps)
    order = sorted(range(B_GROUPS), key=lambda gi: -POOL_WINDOWS[gi])
    halves = (order[:B_GROUPS // 2], order[B_GROUPS // 2:])

    def rows_of(sub):
        return pl.ds(sub * TM, TM)

    def in_proj(sub):
        x = x_ref[rows_of(sub), :]
        xg_ref[sub] = (x * g_ref[...]).astype(jnp.bfloat16)
        r = lax.rsqrt(jnp.mean(x * x, axis=-1, keepdims=True) + EPS)
        r_ref[sub] = jnp.broadcast_to(r, (TM, B_GROUP_DIM))
        carry = ext_ref[(sub - 1) % POOL_SUBTILES, TM:rows_ext, :]
        if sub == 0:
            carry = jnp.where(step_in_seq == 0, 0.0, carry)
        ext_ref[sub, 0:HALO, :] = carry
        for gi in order:
            cols = slice(gi * B_GROUP_DIM, (gi + 1) * B_GROUP_DIM)
            ext_ref[sub, HALO:, cols] = (
                _dot(xg_ref[sub], win_ref[:, cols]) * r_ref[sub])

    def pool_group(sub, gi):
        w = POOL_WINDOWS[gi]
        lo = gi * B_GROUP_DIM
        cols = slice(lo, lo + B_GROUP_DIM)
        pos = ((step_in_seq * POOL_SUBTILES + sub) * TM
               + lax.broadcasted_iota(jnp.int32, (TM, 1), 0))
        src, src_cols, level = ext_ref.at[sub], cols, 0
        k = 1
        while 2 * k < w:
            level += 1
            r0 = 8 * level
            dst = sum_a_ref if level % 2 else sum_b_ref
            dst[r0:rows_ext, :] = (src[r0:rows_ext, src_cols]
                                   + src[r0 - k:rows_ext - k, src_cols])
            src, src_cols = dst, slice(None)
            k *= 2
        acc = src[HALO:rows_ext, src_cols] + src[HALO - k:rows_ext - k, src_cols]
        cnt = jnp.minimum(pos + 1, w).astype(jnp.float32)
        pooled = acc / cnt - ext_ref[sub, HALO:, cols]
        y = _dot(pooled.astype(jnp.bfloat16),
                 wgrp_ref[lo:lo + B_GROUP_DIM, :B_GROUP_DIM])
        act_ref[sub, :, cols] = (y * scale_ref[:, cols]).astype(jnp.bfloat16)

    def half(sub, h):
        for gi in halves[h]:
            pool_group(sub, gi)
        lo = min(halves[h]) * B_GROUP_DIM
        hi = (max(halves[h]) + 1) * B_GROUP_DIM
        part = _dot(act_ref[sub, :, lo:hi], wout_ref[lo:hi, :D_MODEL])
        rows = rows_of(sub)
        if h == 0:
            o_ref[rows, :] = x_ref[rows, :] + part
        else:
            o_ref[rows, :] += part

    in_proj(0)
    for sub in range(POOL_SUBTILES):
        half(sub, 0)
        if sub + 1 < POOL_SUBTILES:
            in_proj(sub + 1)
        half(sub, 1)


def _mixb(x2, seq, g, w_in, w_grp, scale, w_out, casts):
    t, d = x2.shape
    tb = POOL_SUBTILES * TM
    n_steps = t // tb
    c_in, c_out, c_shapes = _cast_specs(casts, n_steps)
    kern = functools.partial(_mixb_kernel, n_cast=len(casts), seq_steps=seq // tb)
    outs = pl.pallas_call(
        kern,
        grid=(n_steps,),
        in_specs=[
            pl.BlockSpec((tb, d), lambda i: (i, 0)),
            _resident((1, d)),
            _resident(w_in.shape),
            _resident(w_grp.shape),
            _resident((1, d)),
            _resident(w_out.shape),
        ] + c_in,
        out_specs=[pl.BlockSpec((tb, d), lambda i: (i, 0))] + c_out,
        out_shape=[jax.ShapeDtypeStruct((t, d), jnp.float32)] + c_shapes,
        scratch_shapes=[
            pltpu.VMEM((POOL_SUBTILES, HALO + TM, d), jnp.float32),
            pltpu.VMEM((HALO + TM, B_GROUP_DIM), jnp.float32),
            pltpu.VMEM((HALO + TM, B_GROUP_DIM), jnp.float32),
            pltpu.VMEM((POOL_SUBTILES, TM, d), jnp.bfloat16),
            pltpu.VMEM((POOL_SUBTILES, TM, B_GROUP_DIM), jnp.float32),
            pltpu.VMEM((POOL_SUBTILES, TM, d), jnp.bfloat16),
        ],
        compiler_params=pltpu.CompilerParams(
            dimension_semantics=("arbitrary",), vmem_limit_bytes=VMEM_LIMIT),
        name="mix_pool",
    )(x2, g, w_in, w_grp, scale, w_out, *[w for w, _ in casts])
    return outs[0], outs[1:]


def _stack3(w):
    return w.reshape((-1,) + w.shape[-2:])


def kernel(x, ffn_norm, ffn_w_in, ffn_w_out, mix_norm, a_w_in, a_v_norm, a_w_s,
           a_b_s, a_w_out, b_w_in, b_w_grp, b_scale, b_w_out, final_norm):
    bsz, seq, d = x.shape
    depth = ffn_norm.shape[0]
    row = lambda v: v.reshape(1, -1)
    fw_in, fw_out = _stack3(ffn_w_in), _stack3(ffn_w_out)
    bw_grp = b_w_grp.reshape(b_w_grp.shape[0], -1, b_w_grp.shape[-1])

    subs = []
    for i in range(depth):
        subs.append(("ffn", i, 0, [(fw_in, 2 * i), (fw_out, 2 * i)]))
        j = i // 2
        if i % 2 == 0:
            subs.append(("sgu", i, j, [(a_w_in, j), (a_w_out, j)]))
        else:
            subs.append(("pool", i, j, [(b_w_in, j), (bw_grp, j), (b_w_out, j)]))
        subs.append(("ffn", i, 1, [(fw_in, 2 * i + 1), (fw_out, 2 * i + 1)]))

    x2 = x.reshape(bsz * seq, d)
    fg = row(final_norm)
    weights = [_cast_padded(w[l]) for w, l in subs[0][3]]
    for n, (kind, i, k, _) in enumerate(subs):
        casts = subs[n + 1][3] if n + 1 < len(subs) else []
        if kind == "ffn":
            x2, nxt = _ffn(x2, row(ffn_norm[i, k]), weights[0], weights[1], fg,
                           casts, final_norm=(n == len(subs) - 1))
        elif kind == "sgu":
            x2, nxt = _mixa(x2, row(mix_norm[i]), weights[0], row(a_v_norm[k]),
                            a_w_s[k], a_b_s[k].T, weights[1], casts)
        else:
            x2, nxt = _mixb(x2, seq, row(mix_norm[i]), weights[0], weights[1],
                            row(b_scale[k]), weights[2], casts)
        weights = list(nxt)
    return x2.reshape(bsz, seq, d)
```

```python
import functools

import jax
import jax.numpy as jnp
from jax import lax
from jax.experimental import pallas as pl
from jax.experimental.pallas import tpu as pltpu

D_MODEL = 1024
D_FF = 2816
CHUNK = 128
A_DIM = 2 * D_MODEL
A_HEADS = 8
A_HEAD_DIM = A_DIM // A_HEADS
POOL_WINDOWS = (2, 4, 8, 16)
B_GROUPS = len(POOL_WINDOWS)
B_GROUP_DIM = D_MODEL // B_GROUPS
EPS = 1e-6
LOG2E = 1.4426950408889634

HALO = 32
MXU_N = 256
LANES = 128
BF16_SUBLANES = 16
VMEM_LIMIT = 56 * 1024 * 1024

TM = 512


def _rmsnorm(x, g):
    ms = jnp.mean(x * x, axis=-1, keepdims=True)
    return x * lax.rsqrt(ms + EPS) * g


def _dot(a, b):
    return jnp.dot(a, b, preferred_element_type=jnp.float32)


def _resident(shape):
    return pl.BlockSpec(shape, lambda *_: (0,) * len(shape),
                        pipeline_mode=pl.Buffered(1))


def _cast_blocks(rows, n_steps):
    best = 1
    for nb in range(1, n_steps + 1):
        if rows % nb == 0 and (rows // nb) % BF16_SUBLANES == 0:
            best = nb
    return best


def _padded_cols(cols):
    return cols + LANES if (cols // LANES) % 8 == 0 else cols


def _cast_padded(w):
    pad = _padded_cols(w.shape[-1]) - w.shape[-1]
    wb = w.astype(jnp.bfloat16)
    if pad == 0:
        return wb
    return jnp.concatenate([wb, jnp.zeros((w.shape[0], pad), jnp.bfloat16)], axis=1)


def _cast_specs(casts, n_steps):
    in_specs, out_specs, out_shapes = [], [], []
    for w, layer in casts:
        _, rows, cols = w.shape
        nb = _cast_blocks(rows, n_steps)
        rb = rows // nb
        in_specs.append(pl.BlockSpec(
            (None, rb, cols),
            lambda i, layer=layer, nb=nb: (layer, jnp.minimum(i, nb - 1), 0)))
        out_specs.append(pl.BlockSpec(
            (rb, _padded_cols(cols)),
            lambda i, nb=nb: (jnp.minimum(i, nb - 1), 0)))
        out_shapes.append(
            jax.ShapeDtypeStruct((rows, _padded_cols(cols)), jnp.bfloat16))
    return in_specs, out_specs, out_shapes


def _run_casts(src_refs, dst_refs):
    for src, dst in zip(src_refs, dst_refs):
        rows, cols = src.shape
        dst[:, :cols] = src[...].astype(jnp.bfloat16)
        if dst.shape[1] > cols:
            dst[:, cols:] = jnp.zeros((rows, dst.shape[1] - cols), jnp.bfloat16)


FFN_SUBTILES = 2


def _ffn_kernel(*refs, n_cast, final_norm):
    x_ref, g_ref, win_ref, wout_ref, fg_ref = refs[:5]
    cast_src = refs[5:5 + n_cast]
    o_ref = refs[5 + n_cast]
    cast_dst = refs[6 + n_cast:6 + 2 * n_cast]
    xg_ref, a_ref, r_ref = refs[6 + 2 * n_cast:]
    _run_casts(cast_src, cast_dst)
    for r0 in range(0, x_ref.shape[0], TM):
        rows = pl.ds(r0, TM)
        x = x_ref[rows, :]
        xg_ref[rows, :] = (x * g_ref[...]).astype(jnp.bfloat16)
        r = lax.rsqrt(jnp.mean(x * x, axis=-1, keepdims=True) + EPS)
        r_ref[rows, :] = jnp.broadcast_to(-LOG2E * r, (TM, MXU_N))
        for c in range(D_FF // MXU_N):
            lo = c * MXU_N
            g_raw = _dot(xg_ref[rows, :], win_ref[:, lo:lo + MXU_N])
            u_raw = _dot(xg_ref[rows, :], win_ref[:, D_FF + lo:D_FF + lo + MXU_N])
            sig = 1.0 / (1.0 + jnp.exp2(g_raw * r_ref[rows, :]))
            a_ref[rows, lo:lo + MXU_N] = (g_raw * u_raw * sig).astype(jnp.bfloat16)
        rl = r_ref[rows, :]
        half_r2 = jnp.concatenate(
            [(0.5 / (LOG2E * LOG2E)) * (rl * rl)] * (D_MODEL // MXU_N), axis=1)
        y = x_ref[rows, :] + half_r2 * _dot(a_ref[rows, :], wout_ref[:, :D_MODEL])
        if final_norm:
            y = _rmsnorm(y, fg_ref[...])
        o_ref[rows, :] = y


def _ffn(x2, g, w_in, w_out, fg, casts, *, final_norm):
    t, d = x2.shape
    tb = FFN_SUBTILES * TM
    n_steps = t // tb
    c_in, c_out, c_shapes = _cast_specs(casts, n_steps)
    kern = functools.partial(_ffn_kernel, n_cast=len(casts), final_norm=final_norm)
    outs = pl.pallas_call(
        kern,
        grid=(n_steps,),
        in_specs=[
            pl.BlockSpec((tb, d), lambda i: (i, 0)),
            _resident((1, d)),
            _resident(w_in.shape),
            _resident(w_out.shape),
            _resident((1, d)),
        ] + c_in,
        out_specs=[pl.BlockSpec((tb, d), lambda i: (i, 0))] + c_out,
        out_shape=[jax.ShapeDtypeStruct((t, d), jnp.float32)] + c_shapes,
        scratch_shapes=[
            pltpu.VMEM((tb, d), jnp.bfloat16),
            pltpu.VMEM((tb, D_FF), jnp.bfloat16),
            pltpu.VMEM((tb, MXU_N), jnp.float32),
        ],
        compiler_params=pltpu.CompilerParams(
            dimension_semantics=("arbitrary",), vmem_limit_bytes=VMEM_LIMIT),
        name="ffn",
    )(x2, g, w_in, w_out, fg, *[w for w, _ in casts])
    return outs[0], outs[1:]


def _gelu_tanh(x):
    k = -2.0 * 0.7978845608028654 * LOG2E
    return x / (1.0 + jnp.exp2(x * (k + (k * 0.044715) * (x * x))))


SGU_SUBTILES = 2
SGU_OUT_BLOCKS = (2, 4)


def _mixa_kernel(*refs, n_cast):
    (x_ref, g_ref, win_ref, vn_ref, ws_ref, bst_ref, wout_ref) = refs[:7]
    cast_src = refs[7:7 + n_cast]
    o_ref = refs[7 + n_cast]
    cast_dst = refs[8 + n_cast:8 + 2 * n_cast]
    h_ref, u_ref, v_ref, act_ref = refs[8 + 2 * n_cast:]
    _run_casts(cast_src, cast_dst)
    col_chunk = MXU_N
    n_chunks = A_DIM // col_chunk
    row_blocks, col_blocks = SGU_OUT_BLOCKS
    row = lax.broadcasted_iota(jnp.int32, (CHUNK, CHUNK), 0)
    col = lax.broadcasted_iota(jnp.int32, (CHUNK, CHUNK), 1)
    causal = col <= row

    def out_blocks(sub):
        def block(r, c):
            height, width = TM // row_blocks, D_MODEL // col_blocks
            rows = pl.ds(sub * TM + r * height, height)
            cols = slice(c * width, (c + 1) * width)
            o_ref[rows, cols] = x_ref[rows, cols] + _dot(
                act_ref[sub, r * height:(r + 1) * height, :], wout_ref[:, cols])
        return [functools.partial(block, r, c)
                for c in range(col_blocks) for r in range(row_blocks)]

    def activations(sub, fillers):
        rows = pl.ds(sub * TM, TM)
        fillers = list(fillers)
        n_after = -(-(len(fillers) - 1) // (n_chunks - 1)) if fillers else 0
        if fillers:
            fillers.pop(0)()
        h_ref[...] = _rmsnorm(x_ref[rows, :], g_ref[...]).astype(jnp.bfloat16)
        ssq = jnp.zeros((TM, 1), jnp.float32)
        for c in range(n_chunks):
            lo = c * col_chunk
            v = _gelu_tanh(
                _dot(h_ref[...], win_ref[:, A_DIM + lo:A_DIM + lo + col_chunk]))
            for _ in range(min(n_after, len(fillers))):
                fillers.pop(0)()
            ssq = ssq + jnp.sum(v * v, axis=-1, keepdims=True)
            v_ref[:, lo:lo + col_chunk] = v
        assert not fillers
        inv = lax.rsqrt(ssq * (1.0 / A_DIM) + EPS)

        def u_proj(hd):
            lo = hd * A_HEAD_DIM
            u_ref[:, lo:lo + A_HEAD_DIM] = _gelu_tanh(
                _dot(h_ref[...], win_ref[:, lo:lo + A_HEAD_DIM]))

        def gate_head(hd):
            lo = hd * A_HEAD_DIM
            ws = jnp.where(causal, ws_ref[hd], 0.0).astype(jnp.bfloat16)
            bias = bst_ref[:, hd:hd + 1]
            vn = (v_ref[:, lo:lo + A_HEAD_DIM] * inv
                  * vn_ref[:, lo:lo + A_HEAD_DIM]).astype(jnp.bfloat16)
            for ck in range(TM // CHUNK):
                r0 = ck * CHUNK
                gate = _dot(ws, vn[r0:r0 + CHUNK, :]) + bias
                act_ref[sub, r0:r0 + CHUNK, lo:lo + A_HEAD_DIM] = (
                    u_ref[r0:r0 + CHUNK, lo:lo + A_HEAD_DIM] * gate
                ).astype(jnp.bfloat16)

        u_proj(0)
        for hd in range(A_HEADS):
            if hd + 1 < A_HEADS:
                u_proj(hd + 1)
            gate_head(hd)

    pending = []
    for sub in range(SGU_SUBTILES):
        activations(sub, pending)
        pending = out_blocks(sub)
    for block in pending:
        block()


def _mixa(x2, g, w_in, v_norm, w_s, b_st, w_out, casts):
    t, d = x2.shape
    tb = SGU_SUBTILES * TM
    n_steps = t // tb
    c_in, c_out, c_shapes = _cast_specs(casts, n_steps)
    kern = functools.partial(_mixa_kernel, n_cast=len(casts))
    outs = pl.pallas_call(
        kern,
        grid=(n_steps,),
        in_specs=[
            pl.BlockSpec((tb, d), lambda i: (i, 0)),
            _resident((1, d)),
            _resident(w_in.shape),
            _resident((1, A_DIM)),
            _resident(w_s.shape),
            _resident(b_st.shape),
            _resident(w_out.shape),
        ] + c_in,
        out_specs=[pl.BlockSpec((tb, d), lambda i: (i, 0))] + c_out,
        out_shape=[jax.ShapeDtypeStruct((t, d), jnp.float32)] + c_shapes,
        scratch_shapes=[
            pltpu.VMEM((TM, d), jnp.bfloat16),
            pltpu.VMEM((TM, A_DIM), jnp.float32),
            pltpu.VMEM((TM, A_DIM), jnp.float32),
            pltpu.VMEM((SGU_SUBTILES, TM, A_DIM), jnp.bfloat16),
        ],
        compiler_params=pltpu.CompilerParams(
            dimension_semantics=("arbitrary",), vmem_limit_bytes=VMEM_LIMIT),
        name="mix_sgu",
    )(x2, g, w_in, v_norm, w_s, b_st, w_out, *[w for w, _ in casts])
    return outs[0], outs[1:]


POOL_SUBTILES = 2


def _mixb_kernel(*refs, n_cast, seq_steps):
    x_ref, g_ref, win_ref, wgrp_ref, scale_ref, wout_ref = refs[:6]
    cast_src = refs[6:6 + n_cast]
    o_ref = refs[6 + n_cast]
    cast_dst = refs[7 + n_cast:7 + 2 * n_cast]
    ext_ref, sum_a_ref, sum_b_ref, xg_ref, r_ref, act_ref = refs[7 + 2 * n_cast:]
    i = pl.program_id(0)
    rows_ext = HALO + TM
    last = POOL_SUBTILES - 1

    @pl.when(i == 0)
    def _():
        ext_ref[last, TM:rows_ext, :] = jnp.zeros((HALO, D_MODEL), jnp.float32)

    _run_casts(cast_src, cast_dst)
    step_in_seq = lax.rem(i, seq_steps)
    order = sorted(range(B_GROUPS), key=lambda gi: -POOL_WINDOWS[gi])
    halves = (order[:B_GROUPS // 2], order[B_GROUPS // 2:])

    def rows_of(sub):
        return pl.ds(sub * TM, TM)

    def in_proj(sub):
        x = x_ref[rows_of(sub), :]
        xg_ref[sub] = (x * g_ref[...]).astype(jnp.bfloat16)
        r = lax.rsqrt(jnp.mean(x * x, axis=-1, keepdims=True) + EPS)
        r_ref[sub] = jnp.broadcast_to(r, (TM, B_GROUP_DIM))
        carry = ext_ref[(sub - 1) % POOL_SUBTILES, TM:rows_ext, :]
        if sub == 0:
            carry = jnp.where(step_in_seq == 0, 0.0, carry)
        ext_ref[sub, 0:HALO, :] = carry
        for gi in order:
            cols = slice(gi * B_GROUP_DIM, (gi + 1) * B_GROUP_DIM)
            ext_ref[sub, HALO:, cols] = (
                _dot(xg_ref[sub], win_ref[:, cols]) * r_ref[sub])

    def pool_group(sub, gi):
        w = POOL_WINDOWS[gi]
        lo = gi * B_GROUP_DIM
        cols = slice(lo, lo + B_GROUP_DIM)
        pos = ((step_in_seq * POOL_SUBTILES + sub) * TM
               + lax.broadcasted_iota(jnp.int32, (TM, 1), 0))
        src, src_cols, level = ext_ref.at[sub], cols, 0
        k = 1
        while 2 * k < w:
            level += 1
            r0 = 8 * level
            dst = sum_a_ref if level % 2 else sum_b_ref
            dst[r0:rows_ext, :] = (src[r0:rows_ext, src_cols]
                                   + src[r0 - k:rows_ext - k, src_cols])
            src, src_cols = dst, slice(None)
            k *= 2
        acc = src[HALO:rows_ext, src_cols] + src[HALO - k:rows_ext - k, src_cols]
        cnt = jnp.minimum(pos + 1, w).astype(jnp.float32)
        pooled = acc / cnt - ext_ref[sub, HALO:, cols]
        y = _dot(pooled.astype(jnp.bfloat16),
                 wgrp_ref[lo:lo + B_GROUP_DIM, :B_GROUP_DIM])
        act_ref[sub, :, cols] = (y * scale_ref[:, cols]).astype(jnp.bfloat16)

    def half(sub, h):
        for gi in halves[h]:
            pool_group(sub, gi)
        lo = min(halves[h]) * B_GROUP_DIM
        hi = (max(halves[h]) + 1) * B_GROUP_DIM
        part = _dot(act_ref[sub, :, lo:hi], wout_ref[lo:hi, :D_MODEL])
        rows = rows_of(sub)
        if h == 0:
            o_ref[rows, :] = x_ref[rows, :] + part
        else:
            o_ref[rows, :] += part

    in_proj(0)
    for sub in range(POOL_SUBTILES):
        half(sub, 0)
        if sub + 1 < POOL_SUBTILES:
            in_proj(sub + 1)
        half(sub, 1)


def _mixb(x2, seq, g, w_in, w_grp, scale, w_out, casts):
    t, d = x2.shape
    tb = POOL_SUBTILES * TM
    n_steps = t // tb
    c_in, c_out, c_shapes = _cast_specs(casts, n_steps)
    kern = functools.partial(_mixb_kernel, n_cast=len(casts), seq_steps=seq // tb)
    outs = pl.pallas_call(
        kern,
        grid=(n_steps,),
        in_specs=[
            pl.BlockSpec((tb, d), lambda i: (i, 0)),
            _resident((1, d)),
            _resident(w_in.shape),
            _resident(w_grp.shape),
            _resident((1, d)),
            _resident(w_out.shape),
        ] + c_in,
        out_specs=[pl.BlockSpec((tb, d), lambda i: (i, 0))] + c_out,
        out_shape=[jax.ShapeDtypeStruct((t, d), jnp.float32)] + c_shapes,
        scratch_shapes=[
            pltpu.VMEM((POOL_SUBTILES, HALO + TM, d), jnp.float32),
            pltpu.VMEM((HALO + TM, B_GROUP_DIM), jnp.float32),
            pltpu.VMEM((HALO + TM, B_GROUP_DIM), jnp.float32),
            pltpu.VMEM((POOL_SUBTILES, TM, d), jnp.bfloat16),
            pltpu.VMEM((POOL_SUBTILES, TM, B_GROUP_DIM), jnp.float32),
            pltpu.VMEM((POOL_SUBTILES, TM, d), jnp.bfloat16),
        ],
        compiler_params=pltpu.CompilerParams(
            dimension_semantics=("arbitrary",), vmem_limit_bytes=VMEM_LIMIT),
        name="mix_pool",
    )(x2, g, w_in, w_grp, scale, w_out, *[w for w, _ in casts])
    return outs[0], outs[1:]


def _stack3(w):
    return w.reshape((-1,) + w.shape[-2:])


def kernel(x, ffn_norm, ffn_w_in, ffn_w_out, mix_norm, a_w_in, a_v_norm, a_w_s,
           a_b_s, a_w_out, b_w_in, b_w_grp, b_scale, b_w_out, final_norm):
    bsz, seq, d = x.shape
    depth = ffn_norm.shape[0]
    row = lambda v: v.reshape(1, -1)
    fw_in, fw_out = _stack3(ffn_w_in), _stack3(ffn_w_out)
    bw_grp = b_w_grp.reshape(b_w_grp.shape[0], -1, b_w_grp.shape[-1])

    subs = []
    for i in range(depth):
        subs.append(("ffn", i, 0, [(fw_in, 2 * i), (fw_out, 2 * i)]))
        j = i // 2
        if i % 2 == 0:
            subs.append(("sgu", i, j, [(a_w_in, j), (a_w_out, j)]))
        else:
            subs.append(("pool", i, j, [(b_w_in, j), (bw_grp, j), (b_w_out, j)]))
        subs.append(("ffn", i, 1, [(fw_in, 2 * i + 1), (fw_out, 2 * i + 1)]))

    x2 = x.reshape(bsz * seq, d)
    fg = row(final_norm)
    weights = [_cast_padded(w[l]) for w, l in subs[0][3]]
    for n, (kind, i, k, _) in enumerate(subs):
        casts = subs[n + 1][3] if n + 1 < len(subs) else []
        if kind == "ffn":
            x2, nxt = _ffn(x2, row(ffn_norm[i, k]), weights[0], weights[1], fg,
                           casts, final_norm=(n == len(subs) - 1))
        elif kind == "sgu":
            x2, nxt = _mixa(x2, row(mix_norm[i]), weights[0], row(a_v_norm[k]),
                            a_w_s[k], a_b_s[k].T, weights[1], casts)
        else:
            x2, nxt = _mixb(x2, seq, row(mix_norm[i]), weights[0], weights[1],
                            row(b_scale[k]), weights[2], casts)
        weights = list(nxt)
    return x2.reshape(bsz, seq, d)
```

```python
import functools

import jax
import jax.numpy as jnp
from jax import lax
from jax.experimental import pallas as pl
from jax.experimental.pallas import tpu as pltpu

D_MODEL = 1024
D_FF = 2816
CHUNK = 128
A_DIM = 2 * D_MODEL
A_HEADS = 8
A_HEAD_DIM = A_DIM // A_HEADS
POOL_WINDOWS = (2, 4, 8, 16)
B_GROUPS = len(POOL_WINDOWS)
B_GROUP_DIM = D_MODEL // B_GROUPS
EPS = 1e-6
LOG2E = 1.4426950408889634

HALO = 32
MXU_N = 256
LANES = 128
BF16_SUBLANES = 16
VMEM_LIMIT = 56 * 1024 * 1024

TM = 512


def _rmsnorm(x, g):
    ms = jnp.mean(x * x, axis=-1, keepdims=True)
    return x * lax.rsqrt(ms + EPS) * g


def _dot(a, b):
    return jnp.dot(a, b, preferred_element_type=jnp.float32)


def _resident(shape):
    return pl.BlockSpec(shape, lambda *_: (0,) * len(shape),
                        pipeline_mode=pl.Buffered(1))


def _cast_blocks(rows, n_steps):
    best = 1
    for nb in range(1, n_steps + 1):
        if rows % nb == 0 and (rows // nb) % BF16_SUBLANES == 0:
            best = nb
    return best


def _padded_cols(cols):
    return cols + LANES if (cols // LANES) % 8 == 0 else cols


def _cast_padded(w):
    pad = _padded_cols(w.shape[-1]) - w.shape[-1]
    wb = w.astype(jnp.bfloat16)
    if pad == 0:
        return wb
    return jnp.concatenate([wb, jnp.zeros((w.shape[0], pad), jnp.bfloat16)], axis=1)


def _cast_specs(casts, n_steps):
    in_specs, out_specs, out_shapes = [], [], []
    for w, layer in casts:
        _, rows, cols = w.shape
        nb = _cast_blocks(rows, n_steps)
        rb = rows // nb
        in_specs.append(pl.BlockSpec(
            (None, rb, cols),
            lambda i, layer=layer, nb=nb: (layer, jnp.minimum(i, nb - 1), 0)))
        out_specs.append(pl.BlockSpec(
            (rb, _padded_cols(cols)),
            lambda i, nb=nb: (jnp.minimum(i, nb - 1), 0)))
        out_shapes.append(
            jax.ShapeDtypeStruct((rows, _padded_cols(cols)), jnp.bfloat16))
    return in_specs, out_specs, out_shapes


def _run_casts(src_refs, dst_refs):
    for src, dst in zip(src_refs, dst_refs):
        rows, cols = src.shape
        dst[:, :cols] = src[...].astype(jnp.bfloat16)
        if dst.shape[1] > cols:
            dst[:, cols:] = jnp.zeros((rows, dst.shape[1] - cols), jnp.bfloat16)


FFN_SUBTILES = 2


def _ffn_kernel(*refs, n_cast, final_norm):
    x_ref, g_ref, win_ref, wout_ref, fg_ref = refs[:5]
    cast_src = refs[5:5 + n_cast]
    o_ref = refs[5 + n_cast]
    cast_dst = refs[6 + n_cast:6 + 2 * n_cast]
    xg_ref, a_ref, r_ref = refs[6 + 2 * n_cast:]
    _run_casts(cast_src, cast_dst)
    for r0 in range(0, x_ref.shape[0], TM):
        rows = pl.ds(r0, TM)
        x = x_ref[rows, :]
        xg_ref[rows, :] = (x * g_ref[...]).astype(jnp.bfloat16)
        r = lax.rsqrt(jnp.mean(x * x, axis=-1, keepdims=True) + EPS)
        r_ref[rows, :] = jnp.broadcast_to(-LOG2E * r, (TM, MXU_N))
        for c in range(D_FF // MXU_N):
            lo = c * MXU_N
            g_raw = _dot(xg_ref[rows, :], win_ref[:, lo:lo + MXU_N])
            u_raw = _dot(xg_ref[rows, :], win_ref[:, D_FF + lo:D_FF + lo + MXU_N])
            sig = 1.0 / (1.0 + jnp.exp2(g_raw * r_ref[rows, :]))
            a_ref[rows, lo:lo + MXU_N] = (g_raw * u_raw * sig).astype(jnp.bfloat16)
        rl = r_ref[rows, :]
        half_r2 = jnp.concatenate(
            [(0.5 / (LOG2E * LOG2E)) * (rl * rl)] * (D_MODEL // MXU_N), axis=1)
        y = x_ref[rows, :] + half_r2 * _dot(a_ref[rows, :], wout_ref[:, :D_MODEL])
        if final_norm:
            y = _rmsnorm(y, fg_ref[...])
        o_ref[rows, :] = y


def _ffn(x2, g, w_in, w_out, fg, casts, *, final_norm):
    t, d = x2.shape
    tb = FFN_SUBTILES * TM
    n_steps = t // tb
    c_in, c_out, c_shapes = _cast_specs(casts, n_steps)
    kern = functools.partial(_ffn_kernel, n_cast=len(casts), final_norm=final_norm)
    outs = pl.pallas_call(
        kern,
        grid=(n_steps,),
        in_specs=[
            pl.BlockSpec((tb, d), lambda i: (i, 0)),
            _resident((1, d)),
            _resident(w_in.shape),
            _resident(w_out.shape),
            _resident((1, d)),
        ] + c_in,
        out_specs=[pl.BlockSpec((tb, d), lambda i: (i, 0))] + c_out,
        out_shape=[jax.ShapeDtypeStruct((t, d), jnp.float32)] + c_shapes,
        scratch_shapes=[
            pltpu.VMEM((tb, d), jnp.bfloat16),
            pltpu.VMEM((tb, D_FF), jnp.bfloat16),
            pltpu.VMEM((tb, MXU_N), jnp.float32),
        ],
        compiler_params=pltpu.CompilerParams(
            dimension_semantics=("arbitrary",), vmem_limit_bytes=VMEM_LIMIT),
        name="ffn",
    )(x2, g, w_in, w_out, fg, *[w for w, _ in casts])
    return outs[0], outs[1:]


def _gelu_tanh(x):
    k = -2.0 * 0.7978845608028654 * LOG2E
    return x / (1.0 + jnp.exp2(x * (k + (k * 0.044715) * (x * x))))


SGU_TM = 256
SGU_SUBTILES = 4
SGU_OUT_BLOCKS = (1, 4)


def _mixa_kernel(*refs, n_cast):
    (x_ref, g_ref, win_ref, vn_ref, ws_ref, bst_ref, wout_ref) = refs[:7]
    cast_src = refs[7:7 + n_cast]
    o_ref = refs[7 + n_cast]
    cast_dst = refs[8 + n_cast:8 + 2 * n_cast]
    h_ref, u_ref, v_ref, act_ref = refs[8 + 2 * n_cast:]
    _run_casts(cast_src, cast_dst)
    col_chunk = MXU_N
    n_chunks = A_DIM // col_chunk
    row_blocks, col_blocks = SGU_OUT_BLOCKS
    row = lax.broadcasted_iota(jnp.int32, (CHUNK, CHUNK), 0)
    col = lax.broadcasted_iota(jnp.int32, (CHUNK, CHUNK), 1)
    causal = col <= row

    def out_blocks(sub):
        def block(r, c):
            height, width = SGU_TM // row_blocks, D_MODEL // col_blocks
            rows = pl.ds(sub * SGU_TM + r * height, height)
            cols = slice(c * width, (c + 1) * width)
            o_ref[rows, cols] = x_ref[rows, cols] + _dot(
                act_ref[sub, r * height:(r + 1) * height, :], wout_ref[:, cols])
        return [functools.partial(block, r, c)
                for c in range(col_blocks) for r in range(row_blocks)]

    def activations(sub, fillers):
        rows = pl.ds(sub * SGU_TM, SGU_TM)
        fillers = list(fillers)
        n_after = -(-(len(fillers) - 1) // (n_chunks - 1)) if fillers else 0
        if fillers:
            fillers.pop(0)()
        h_ref[...] = _rmsnorm(x_ref[rows, :], g_ref[...]).astype(jnp.bfloat16)
        ssq = jnp.zeros((SGU_TM, 1), jnp.float32)
        for c in range(n_chunks):
            lo = c * col_chunk
            v = _gelu_tanh(
                _dot(h_ref[...], win_ref[:, A_DIM + lo:A_DIM + lo + col_chunk]))
            for _ in range(min(n_after, len(fillers))):
                fillers.pop(0)()
            ssq = ssq + jnp.sum(v * v, axis=-1, keepdims=True)
            v_ref[:, lo:lo + col_chunk] = v
        assert not fillers
        inv = lax.rsqrt(ssq * (1.0 / A_DIM) + EPS)

        def u_proj(hd):
            lo = hd * A_HEAD_DIM
            u_ref[:, lo:lo + A_HEAD_DIM] = _gelu_tanh(
                _dot(h_ref[...], win_ref[:, lo:lo + A_HEAD_DIM]))

        def gate_head(hd):
            lo = hd * A_HEAD_DIM
            ws = jnp.where(causal, ws_ref[hd], 0.0).astype(jnp.bfloat16)
            bias = bst_ref[:, hd:hd + 1]
            vn = (v_ref[:, lo:lo + A_HEAD_DIM] * inv
                  * vn_ref[:, lo:lo + A_HEAD_DIM]).astype(jnp.bfloat16)
            for ck in range(SGU_TM // CHUNK):
                r0 = ck * CHUNK
                gate = _dot(ws, vn[r0:r0 + CHUNK, :]) + bias
                act_ref[sub, r0:r0 + CHUNK, lo:lo + A_HEAD_DIM] = (
                    u_ref[r0:r0 + CHUNK, lo:lo + A_HEAD_DIM] * gate
                ).astype(jnp.bfloat16)

        u_proj(0)
        for hd in range(A_HEADS):
            if hd + 1 < A_HEADS:
                u_proj(hd + 1)
            gate_head(hd)

    pending = []
    for sub in range(SGU_SUBTILES):
        activations(sub, pending)
        pending = out_blocks(sub)
    for block in pending:
        block()


def _mixa(x2, g, w_in, v_norm, w_s, b_st, w_out, casts):
    t, d = x2.shape
    tb = SGU_SUBTILES * SGU_TM
    n_steps = t // tb
    c_in, c_out, c_shapes = _cast_specs(casts, n_steps)
    kern = functools.partial(_mixa_kernel, n_cast=len(casts))
    outs = pl.pallas_call(
        kern,
        grid=(n_steps,),
        in_specs=[
            pl.BlockSpec((tb, d), lambda i: (i, 0)),
            _resident((1, d)),
            _resident(w_in.shape),
            _resident((1, A_DIM)),
            _resident(w_s.shape),
            _resident(b_st.shape),
            _resident(w_out.shape),
        ] + c_in,
        out_specs=[pl.BlockSpec((tb, d), lambda i: (i, 0))] + c_out,
        out_shape=[jax.ShapeDtypeStruct((t, d), jnp.float32)] + c_shapes,
        scratch_shapes=[
            pltpu.VMEM((SGU_TM, d), jnp.bfloat16),
            pltpu.VMEM((SGU_TM, A_DIM), jnp.float32),
            pltpu.VMEM((SGU_TM, A_DIM), jnp.float32),
            pltpu.VMEM((SGU_SUBTILES, SGU_TM, A_DIM), jnp.bfloat16),
        ],
        compiler_params=pltpu.CompilerParams(
            dimension_semantics=("arbitrary",), vmem_limit_bytes=VMEM_LIMIT),
        name="mix_sgu",
    )(x2, g, w_in, v_norm, w_s, b_st, w_out, *[w for w, _ in casts])
    return outs[0], outs[1:]


POOL_SUBTILES = 2


def _mixb_kernel(*refs, n_cast, seq_steps):
    x_ref, g_ref, win_ref, wgrp_ref, scale_ref, wout_ref = refs[:6]
    cast_src = refs[6:6 + n_cast]
    o_ref = refs[6 + n_cast]
    cast_dst = refs[7 + n_cast:7 + 2 * n_cast]
    ext_ref, sum_a_ref, sum_b_ref, xg_ref, r_ref, act_ref = refs[7 + 2 * n_cast:]
    i = pl.program_id(0)
    rows_ext = HALO + TM
    last = POOL_SUBTILES - 1

    @pl.when(i == 0)
    def _():
        ext_ref[last, TM:rows_ext, :] = jnp.zeros((HALO, D_MODEL), jnp.float32)

    _run_casts(cast_src, cast_dst)
    step_in_seq = lax.rem(i, seq_steps)
    order = sorted(range(B_GROUPS), key=lambda gi: -POOL_WINDOWS[gi])
    halves = (order[:B_GROUPS // 2], order[B_GROUPS // 2:])

    def rows_of(sub):
        return pl.ds(sub * TM, TM)

    def in_proj(sub):
        x = x_ref[rows_of(sub), :]
        xg_ref[sub] = (x * g_ref[...]).astype(jnp.bfloat16)
        r = lax.rsqrt(jnp.mean(x * x, axis=-1, keepdims=True) + EPS)
        r_ref[sub] = jnp.broadcast_to(r, (TM, B_GROUP_DIM))
        carry = ext_ref[(sub - 1) % POOL_SUBTILES, TM:rows_ext, :]
        if sub == 0:
            carry = jnp.where(step_in_seq == 0, 0.0, carry)
        ext_ref[sub, 0:HALO, :] = carry
        for gi in order:
            cols = slice(gi * B_GROUP_DIM, (gi + 1) * B_GROUP_DIM)
            ext_ref[sub, HALO:, cols] = (
                _dot(xg_ref[sub], win_ref[:, cols]) * r_ref[sub])

    def pool_group(sub, gi):
        w = POOL_WINDOWS[gi]
        lo = gi * B_GROUP_DIM
        cols = slice(lo, lo + B_GROUP_DIM)
        pos = ((step_in_seq * POOL_SUBTILES + sub) * TM
               + lax.broadcasted_iota(jnp.int32, (TM, 1), 0))
        src, src_cols, level = ext_ref.at[sub], cols, 0
        k = 1
        while 2 * k < w:
            level += 1
            r0 = 8 * level
            dst = sum_a_ref if level % 2 else sum_b_ref
            dst[r0:rows_ext, :] = (src[r0:rows_ext, src_cols]
                                   + src[r0 - k:rows_ext - k, src_cols])
            src, src_cols = dst, slice(None)
            k *= 2
        acc = src[HALO:rows_ext, src_cols] + src[HALO - k:rows_ext - k, src_cols]
        cnt = jnp.minimum(pos + 1, w).astype(jnp.float32)
        pooled = acc / cnt - ext_ref[sub, HALO:, cols]
        y = _dot(pooled.astype(jnp.bfloat16),
                 wgrp_ref[lo:lo + B_GROUP_DIM, :B_GROUP_DIM])
        act_ref[sub, :, cols] = (y * scale_ref[:, cols]).astype(jnp.bfloat16)

    def half(sub, h):
        for gi in halves[h]:
            pool_group(sub, gi)
        lo = min(halves[h]) * B_GROUP_DIM
        hi = (max(halves[h]) + 1) * B_GROUP_DIM
        part = _dot(act_ref[sub, :, lo:hi], wout_ref[lo:hi, :D_MODEL])
        rows = rows_of(sub)
        if h == 0:
            o_ref[rows, :] = x_ref[rows, :] + part
        else:
            o_ref[rows, :] += part

    in_proj(0)
    for sub in range(POOL_SUBTILES):
        half(sub, 0)
        if sub + 1 < POOL_SUBTILES:
            in_proj(sub + 1)
        half(sub, 1)


def _mixb(x2, seq, g, w_in, w_grp, scale, w_out, casts):
    t, d = x2.shape
    tb = POOL_SUBTILES * TM
    n_steps = t // tb
    c_in, c_out, c_shapes = _cast_specs(casts, n_steps)
    kern = functools.partial(_mixb_kernel, n_cast=len(casts), seq_steps=seq // tb)
    outs = pl.pallas_call(
        kern,
        grid=(n_steps,),
        in_specs=[
            pl.BlockSpec((tb, d), lambda i: (i, 0)),
            _resident((1, d)),
            _resident(w_in.shape),
            _resident(w_grp.shape),
            _resident((1, d)),
            _resident(w_out.shape),
        ] + c_in,
        out_specs=[pl.BlockSpec((tb, d), lambda i: (i, 0))] + c_out,
        out_shape=[jax.ShapeDtypeStruct((t, d), jnp.float32)] + c_shapes,
        scratch_shapes=[
            pltpu.VMEM((POOL_SUBTILES, HALO + TM, d), jnp.float32),
            pltpu.VMEM((HALO + TM, B_GROUP_DIM), jnp.float32),
            pltpu.VMEM((HALO + TM, B_GROUP_DIM), jnp.float32),
            pltpu.VMEM((POOL_SUBTILES, TM, d), jnp.bfloat16),
            pltpu.VMEM((POOL_SUBTILES, TM, B_GROUP_DIM), jnp.float32),
            pltpu.VMEM((POOL_SUBTILES, TM, d), jnp.bfloat16),
        ],
        compiler_params=pltpu.CompilerParams(
            dimension_semantics=("arbitrary",), vmem_limit_bytes=VMEM_LIMIT),
        name="mix_pool",
    )(x2, g, w_in, w_grp, scale, w_out, *[w for w, _ in casts])
    return outs[0], outs[1:]


def _stack3(w):
    return w.reshape((-1,) + w.shape[-2:])


def kernel(x, ffn_norm, ffn_w_in, ffn_w_out, mix_norm, a_w_in, a_v_norm, a_w_s,
           a_b_s, a_w_out, b_w_in, b_w_grp, b_scale, b_w_out, final_norm):
    bsz, seq, d = x.shape
    depth = ffn_norm.shape[0]
    row = lambda v: v.reshape(1, -1)
    fw_in, fw_out = _stack3(ffn_w_in), _stack3(ffn_w_out)
    bw_grp = b_w_grp.reshape(b_w_grp.shape[0], -1, b_w_grp.shape[-1])

    subs = []
    for i in range(depth):
        subs.append(("ffn", i, 0, [(fw_in, 2 * i), (fw_out, 2 * i)]))
        j = i // 2
        if i % 2 == 0:
            subs.append(("sgu", i, j, [(a_w_in, j), (a_w_out, j)]))
        else:
            subs.append(("pool", i, j, [(b_w_in, j), (bw_grp, j), (b_w_out, j)]))
        subs.append(("ffn", i, 1, [(fw_in, 2 * i + 1), (fw_out, 2 * i + 1)]))

    x2 = x.reshape(bsz * seq, d)
    fg = row(final_norm)
    weights = [_cast_padded(w[l]) for w, l in subs[0][3]]
    for n, (kind, i, k, _) in enumerate(subs):
        casts = subs[n + 1][3] if n + 1 < len(subs) else []
        if kind == "ffn":
            x2, nxt = _ffn(x2, row(ffn_norm[i, k]), weights[0], weights[1], fg,
                           casts, final_norm=(n == len(subs) - 1))
        elif kind == "sgu":
            x2, nxt = _mixa(x2, row(mix_norm[i]), weights[0], row(a_v_norm[k]),
                            a_w_s[k], a_b_s[k].T, weights[1], casts)
        else:
            x2, nxt = _mixb(x2, seq, row(mix_norm[i]), weights[0], weights[1],
                            row(b_scale[k]), weights[2], casts)
        weights = list(nxt)
    return x2.reshape(bsz, seq, d)
```

```python
import functools

import jax
import jax.numpy as jnp
from jax import lax
from jax.experimental import pallas as pl
from jax.experimental.pallas import tpu as pltpu

D_MODEL = 1024
D_FF = 2816
CHUNK = 128
A_DIM = 2 * D_MODEL
A_HEADS = 8
A_HEAD_DIM = A_DIM // A_HEADS
POOL_WINDOWS = (2, 4, 8, 16)
B_GROUPS = len(POOL_WINDOWS)
B_GROUP_DIM = D_MODEL // B_GROUPS
EPS = 1e-6
LOG2E = 1.4426950408889634

HALO = 32
MXU_N = 256
LANES = 128
BF16_SUBLANES = 16
VMEM_LIMIT = 56 * 1024 * 1024

TM = 512


def _rmsnorm(x, g):
    ms = jnp.mean(x * x, axis=-1, keepdims=True)
    return x * lax.rsqrt(ms + EPS) * g


def _dot(a, b):
    return jnp.dot(a, b, preferred_element_type=jnp.float32)


def _resident(shape):
    return pl.BlockSpec(shape, lambda *_: (0,) * len(shape),
                        pipeline_mode=pl.Buffered(1))


def _cast_blocks(rows, n_steps):
    best = 1
    for nb in range(1, n_steps + 1):
        if rows % nb == 0 and (rows // nb) % BF16_SUBLANES == 0:
            best = nb
    return best


def _padded_cols(cols):
    return cols + LANES if (cols // LANES) % 8 == 0 else cols


def _cast_padded(w):
    pad = _padded_cols(w.shape[-1]) - w.shape[-1]
    wb = w.astype(jnp.bfloat16)
    if pad == 0:
        return wb
    return jnp.concatenate([wb, jnp.zeros((w.shape[0], pad), jnp.bfloat16)], axis=1)


def _cast_specs(casts, n_steps):
    in_specs, out_specs, out_shapes = [], [], []
    for w, layer in casts:
        _, rows, cols = w.shape
        nb = _cast_blocks(rows, n_steps)
        rb = rows // nb
        in_specs.append(pl.BlockSpec(
            (None, rb, cols),
            lambda i, layer=layer, nb=nb: (layer, jnp.minimum(i, nb - 1), 0)))
        out_specs.append(pl.BlockSpec(
            (rb, _padded_cols(cols)),
            lambda i, nb=nb: (jnp.minimum(i, nb - 1), 0)))
        out_shapes.append(
            jax.ShapeDtypeStruct((rows, _padded_cols(cols)), jnp.bfloat16))
    return in_specs, out_specs, out_shapes


def _run_casts(src_refs, dst_refs):
    for src, dst in zip(src_refs, dst_refs):
        rows, cols = src.shape
        dst[:, :cols] = src[...].astype(jnp.bfloat16)
        if dst.shape[1] > cols:
            dst[:, cols:] = jnp.zeros((rows, dst.shape[1] - cols), jnp.bfloat16)


FFN_SUBTILES = 2


def _ffn_kernel(*refs, n_cast, final_norm):
    x_ref, g_ref, win_ref, wout_ref, fg_ref = refs[:5]
    cast_src = refs[5:5 + n_cast]
    o_ref = refs[5 + n_cast]
    cast_dst = refs[6 + n_cast:6 + 2 * n_cast]
    xg_ref, a_ref, r_ref = refs[6 + 2 * n_cast:]
    _run_casts(cast_src, cast_dst)
    for r0 in range(0, x_ref.shape[0], TM):
        rows = pl.ds(r0, TM)
        x = x_ref[rows, :]
        xg_ref[rows, :] = (x * g_ref[...]).astype(jnp.bfloat16)
        r = lax.rsqrt(jnp.mean(x * x, axis=-1, keepdims=True) + EPS)
        r_ref[rows, :] = jnp.broadcast_to(-LOG2E * r, (TM, MXU_N))
        for c in range(D_FF // MXU_N):
            lo = c * MXU_N
            g_raw = _dot(xg_ref[rows, :], win_ref[:, lo:lo + MXU_N])
            u_raw = _dot(xg_ref[rows, :], win_ref[:, D_FF + lo:D_FF + lo + MXU_N])
            sig = 1.0 / (1.0 + jnp.exp2(g_raw * r_ref[rows, :]))
            a_ref[rows, lo:lo + MXU_N] = (g_raw * u_raw * sig).astype(jnp.bfloat16)
        rl = r_ref[rows, :]
        half_r2 = jnp.concatenate(
            [(0.5 / (LOG2E * LOG2E)) * (rl * rl)] * (D_MODEL // MXU_N), axis=1)
        y = x_ref[rows, :] + half_r2 * _dot(a_ref[rows, :], wout_ref[:, :D_MODEL])
        if final_norm:
            y = _rmsnorm(y, fg_ref[...])
        o_ref[rows, :] = y


def _ffn(x2, g, w_in, w_out, fg, casts, *, final_norm):
    t, d = x2.shape
    tb = FFN_SUBTILES * TM
    n_steps = t // tb
    c_in, c_out, c_shapes = _cast_specs(casts, n_steps)
    kern = functools.partial(_ffn_kernel, n_cast=len(casts), final_norm=final_norm)
    outs = pl.pallas_call(
        kern,
        grid=(n_steps,),
        in_specs=[
            pl.BlockSpec((tb, d), lambda i: (i, 0)),
            _resident((1, d)),
            _resident(w_in.shape),
            _resident(w_out.shape),
            _resident((1, d)),
        ] + c_in,
        out_specs=[pl.BlockSpec((tb, d), lambda i: (i, 0))] + c_out,
        out_shape=[jax.ShapeDtypeStruct((t, d), jnp.float32)] + c_shapes,
        scratch_shapes=[
            pltpu.VMEM((tb, d), jnp.bfloat16),
            pltpu.VMEM((tb, D_FF), jnp.bfloat16),
            pltpu.VMEM((tb, MXU_N), jnp.float32),
        ],
        compiler_params=pltpu.CompilerParams(
            dimension_semantics=("arbitrary",), vmem_limit_bytes=VMEM_LIMIT),
        name="ffn",
    )(x2, g, w_in, w_out, fg, *[w for w, _ in casts])
    return outs[0], outs[1:]


def _gelu_tanh(x):
    k = -2.0 * 0.7978845608028654 * LOG2E
    return x / (1.0 + jnp.exp2(x * (k + (k * 0.044715) * (x * x))))


SGU_TM = 256
SGU_SUBTILES = 4
SGU_OUT_BLOCKS = (1, 4)


def _mixa_kernel(*refs, n_cast):
    (x_ref, g_ref, win_ref, vn_ref, ws_ref, bst_ref, wout_ref) = refs[:7]
    cast_src = refs[7:7 + n_cast]
    o_ref = refs[7 + n_cast]
    cast_dst = refs[8 + n_cast:8 + 2 * n_cast]
    h_ref, u_ref, v_ref, act_ref = refs[8 + 2 * n_cast:]
    _run_casts(cast_src, cast_dst)
    col_chunk = 2 * MXU_N
    n_chunks = A_DIM // col_chunk
    row_blocks, col_blocks = SGU_OUT_BLOCKS
    row = lax.broadcasted_iota(jnp.int32, (CHUNK, CHUNK), 0)
    col = lax.broadcasted_iota(jnp.int32, (CHUNK, CHUNK), 1)
    causal = col <= row

    def out_blocks(sub):
        def block(r, c):
            height, width = SGU_TM // row_blocks, D_MODEL // col_blocks
            rows = pl.ds(sub * SGU_TM + r * height, height)
            cols = slice(c * width, (c + 1) * width)
            o_ref[rows, cols] = x_ref[rows, cols] + _dot(
                act_ref[sub, r * height:(r + 1) * height, :], wout_ref[:, cols])
        return [functools.partial(block, r, c)
                for c in range(col_blocks) for r in range(row_blocks)]

    def activations(sub, fillers):
        rows = pl.ds(sub * SGU_TM, SGU_TM)
        fillers = list(fillers)
        n_after = -(-(len(fillers) - 1) // (n_chunks - 1)) if fillers else 0
        if fillers:
            fillers.pop(0)()
        h_ref[...] = _rmsnorm(x_ref[rows, :], g_ref[...]).astype(jnp.bfloat16)
        ssq = jnp.zeros((SGU_TM, 1), jnp.float32)
        for c in range(n_chunks):
            lo = c * col_chunk
            v = _gelu_tanh(
                _dot(h_ref[...], win_ref[:, A_DIM + lo:A_DIM + lo + col_chunk]))
            for _ in range(min(n_after, len(fillers))):
                fillers.pop(0)()
            ssq = ssq + jnp.sum(v * v, axis=-1, keepdims=True)
            v_ref[:, lo:lo + col_chunk] = v
        assert not fillers
        inv = lax.rsqrt(ssq * (1.0 / A_DIM) + EPS)

        def u_proj(hd):
            lo = hd * A_HEAD_DIM
            u_ref[:, lo:lo + A_HEAD_DIM] = _gelu_tanh(
                _dot(h_ref[...], win_ref[:, lo:lo + A_HEAD_DIM]))

        def gate_head(hd):
            lo = hd * A_HEAD_DIM
            ws = jnp.where(causal, ws_ref[hd], 0.0).astype(jnp.bfloat16)
            bias = bst_ref[:, hd:hd + 1]
            vn = (v_ref[:, lo:lo + A_HEAD_DIM] * inv
                  * vn_ref[:, lo:lo + A_HEAD_DIM]).astype(jnp.bfloat16)
            for ck in range(SGU_TM // CHUNK):
                r0 = ck * CHUNK
                gate = _dot(ws, vn[r0:r0 + CHUNK, :]) + bias
                act_ref[sub, r0:r0 + CHUNK, lo:lo + A_HEAD_DIM] = (
                    u_ref[r0:r0 + CHUNK, lo:lo + A_HEAD_DIM] * gate
                ).astype(jnp.bfloat16)

        u_proj(0)
        for hd in range(A_HEADS):
            if hd + 1 < A_HEADS:
                u_proj(hd + 1)
            gate_head(hd)

    pending = []
    for sub in range(SGU_SUBTILES):
        activations(sub, pending)
        pending = out_blocks(sub)
    for block in pending:
        block()


def _mixa(x2, g, w_in, v_norm, w_s, b_st, w_out, casts):
    t, d = x2.shape
    tb = SGU_SUBTILES * SGU_TM
    n_steps = t // tb
    c_in, c_out, c_shapes = _cast_specs(casts, n_steps)
    kern = functools.partial(_mixa_kernel, n_cast=len(casts))
    outs = pl.pallas_call(
        kern,
        grid=(n_steps,),
        in_specs=[
            pl.BlockSpec((tb, d), lambda i: (i, 0)),
            _resident((1, d)),
            _resident(w_in.shape),
            _resident((1, A_DIM)),
            _resident(w_s.shape),
            _resident(b_st.shape),
            _resident(w_out.shape),
        ] + c_in,
        out_specs=[pl.BlockSpec((tb, d), lambda i: (i, 0))] + c_out,
        out_shape=[jax.ShapeDtypeStruct((t, d), jnp.float32)] + c_shapes,
        scratch_shapes=[
            pltpu.VMEM((SGU_TM, d), jnp.bfloat16),
            pltpu.VMEM((SGU_TM, A_DIM), jnp.float32),
            pltpu.VMEM((SGU_TM, A_DIM), jnp.float32),
            pltpu.VMEM((SGU_SUBTILES, SGU_TM, A_DIM), jnp.bfloat16),
        ],
        compiler_params=pltpu.CompilerParams(
            dimension_semantics=("arbitrary",), vmem_limit_bytes=VMEM_LIMIT),
        name="mix_sgu",
    )(x2, g, w_in, v_norm, w_s, b_st, w_out, *[w for w, _ in casts])
    return outs[0], outs[1:]


POOL_SUBTILES = 2


def _mixb_kernel(*refs, n_cast, seq_steps):
    x_ref, g_ref, win_ref, wgrp_ref, scale_ref, wout_ref = refs[:6]
    cast_src = refs[6:6 + n_cast]
    o_ref = refs[6 + n_cast]
    cast_dst = refs[7 + n_cast:7 + 2 * n_cast]
    ext_ref, sum_a_ref, sum_b_ref, xg_ref, r_ref, act_ref = refs[7 + 2 * n_cast:]
    i = pl.program_id(0)
    rows_ext = HALO + TM
    last = POOL_SUBTILES - 1

    @pl.when(i == 0)
    def _():
        ext_ref[last, TM:rows_ext, :] = jnp.zeros((HALO, D_MODEL), jnp.float32)

    _run_casts(cast_src, cast_dst)
    step_in_seq = lax.rem(i, seq_steps)
    order = sorted(range(B_GROUPS), key=lambda gi: -POOL_WINDOWS[gi])
    halves = (order[:B_GROUPS // 2], order[B_GROUPS // 2:])

    def rows_of(sub):
        return pl.ds(sub * TM, TM)

    def in_proj(sub):
        x = x_ref[rows_of(sub), :]
        xg_ref[sub] = (x * g_ref[...]).astype(jnp.bfloat16)
        r = lax.rsqrt(jnp.mean(x * x, axis=-1, keepdims=True) + EPS)
        r_ref[sub] = jnp.broadcast_to(r, (TM, B_GROUP_DIM))
        carry = ext_ref[(sub - 1) % POOL_SUBTILES, TM:rows_ext, :]
        if sub == 0:
            carry = jnp.where(step_in_seq == 0, 0.0, carry)
        ext_ref[sub, 0:HALO, :] = carry
        for gi in order:
            cols = slice(gi * B_GROUP_DIM, (gi + 1) * B_GROUP_DIM)
            ext_ref[sub, HALO:, cols] = (
                _dot(xg_ref[sub], win_ref[:, cols]) * r_ref[sub])

    def pool_group(sub, gi):
        w = POOL_WINDOWS[gi]
        lo = gi * B_GROUP_DIM
        cols = slice(lo, lo + B_GROUP_DIM)
        pos = ((step_in_seq * POOL_SUBTILES + sub) * TM
               + lax.broadcasted_iota(jnp.int32, (TM, 1), 0))
        src, src_cols, level = ext_ref.at[sub], cols, 0
        k = 1
        while 2 * k < w:
            level += 1
            r0 = 8 * level
            dst = sum_a_ref if level % 2 else sum_b_ref
            dst[r0:rows_ext, :] = (src[r0:rows_ext, src_cols]
                                   + src[r0 - k:rows_ext - k, src_cols])
            src, src_cols = dst, slice(None)
            k *= 2
        acc = src[HALO:rows_ext, src_cols] + src[HALO - k:rows_ext - k, src_cols]
        cnt = jnp.minimum(pos + 1, w).astype(jnp.float32)
        pooled = acc / cnt - ext_ref[sub, HALO:, cols]
        y = _dot(pooled.astype(jnp.bfloat16),
                 wgrp_ref[lo:lo + B_GROUP_DIM, :B_GROUP_DIM])
        act_ref[sub, :, cols] = (y * scale_ref[:, cols]).astype(jnp.bfloat16)

    def half(sub, h):
        for gi in halves[h]:
            pool_group(sub, gi)
        lo = min(halves[h]) * B_GROUP_DIM
        hi = (max(halves[h]) + 1) * B_GROUP_DIM
        part = _dot(act_ref[sub, :, lo:hi], wout_ref[lo:hi, :D_MODEL])
        rows = rows_of(sub)
        if h == 0:
            o_ref[rows, :] = x_ref[rows, :] + part
        else:
            o_ref[rows, :] += part

    in_proj(0)
    for sub in range(POOL_SUBTILES):
        half(sub, 0)
        if sub + 1 < POOL_SUBTILES:
            in_proj(sub + 1)
        half(sub, 1)


def _mixb(x2, seq, g, w_in, w_grp, scale, w_out, casts):
    t, d = x2.shape
    tb = POOL_SUBTILES * TM
    n_steps = t // tb
    c_in, c_out, c_shapes = _cast_specs(casts, n_steps)
    kern = functools.partial(_mixb_kernel, n_cast=len(casts), seq_steps=seq // tb)
    outs = pl.pallas_call(
        kern,
        grid=(n_steps,),
        in_specs=[
            pl.BlockSpec((tb, d), lambda i: (i, 0)),
            _resident((1, d)),
            _resident(w_in.shape),
            _resident(w_grp.shape),
            _resident((1, d)),
            _resident(w_out.shape),
        ] + c_in,
        out_specs=[pl.BlockSpec((tb, d), lambda i: (i, 0))] + c_out,
        out_shape=[jax.ShapeDtypeStruct((t, d), jnp.float32)] + c_shapes,
        scratch_shapes=[
            pltpu.VMEM((POOL_SUBTILES, HALO + TM, d), jnp.float32),
            pltpu.VMEM((HALO + TM, B_GROUP_DIM), jnp.float32),
            pltpu.VMEM((HALO + TM, B_GROUP_DIM), jnp.float32),
            pltpu.VMEM((POOL_SUBTILES, TM, d), jnp.bfloat16),
            pltpu.VMEM((POOL_SUBTILES, TM, B_GROUP_DIM), jnp.float32),
            pltpu.VMEM((POOL_SUBTILES, TM, d), jnp.bfloat16),
        ],
        compiler_params=pltpu.CompilerParams(
            dimension_semantics=("arbitrary",), vmem_limit_bytes=VMEM_LIMIT),
        name="mix_pool",
    )(x2, g, w_in, w_grp, scale, w_out, *[w for w, _ in casts])
    return outs[0], outs[1:]


def _stack3(w):
    return w.reshape((-1,) + w.shape[-2:])


def kernel(x, ffn_norm, ffn_w_in, ffn_w_out, mix_norm, a_w_in, a_v_norm, a_w_s,
           a_b_s, a_w_out, b_w_in, b_w_grp, b_scale, b_w_out, final_norm):
    bsz, seq, d = x.shape
    depth = ffn_norm.shape[0]
    row = lambda v: v.reshape(1, -1)
    fw_in, fw_out = _stack3(ffn_w_in), _stack3(ffn_w_out)
    bw_grp = b_w_grp.reshape(b_w_grp.shape[0], -1, b_w_grp.shape[-1])

    subs = []
    for i in range(depth):
        subs.append(("ffn", i, 0, [(fw_in, 2 * i), (fw_out, 2 * i)]))
        j = i // 2
        if i % 2 == 0:
            subs.append(("sgu", i, j, [(a_w_in, j), (a_w_out, j)]))
        else:
            subs.append(("pool", i, j, [(b_w_in, j), (bw_grp, j), (b_w_out, j)]))
        subs.append(("ffn", i, 1, [(fw_in, 2 * i + 1), (fw_out, 2 * i + 1)]))

    x2 = x.reshape(bsz * seq, d)
    fg = row(final_norm)
    weights = [_cast_padded(w[l]) for w, l in subs[0][3]]
    for n, (kind, i, k, _) in enumerate(subs):
        casts = subs[n + 1][3] if n + 1 < len(subs) else []
        if kind == "ffn":
            x2, nxt = _ffn(x2, row(ffn_norm[i, k]), weights[0], weights[1], fg,
                           casts, final_norm=(n == len(subs) - 1))
        elif kind == "sgu":
            x2, nxt = _mixa(x2, row(mix_norm[i]), weights[0], row(a_v_norm[k]),
                            a_w_s[k], a_b_s[k].T, weights[1], casts)
        else:
            x2, nxt = _mixb(x2, seq, row(mix_norm[i]), weights[0], weights[1],
                            row(b_scale[k]), weights[2], casts)
        weights = list(nxt)
    return x2.reshape(bsz, seq, d)
```

```python
import functools

import jax
import jax.numpy as jnp
from jax import lax
from jax.experimental import pallas as pl
from jax.experimental.pallas import tpu as pltpu

D_MODEL = 1024
D_FF = 2816
CHUNK = 128
A_DIM = 2 * D_MODEL
A_HEADS = 8
A_HEAD_DIM = A_DIM // A_HEADS
POOL_WINDOWS = (2, 4, 8, 16)
B_GROUPS = len(POOL_WINDOWS)
B_GROUP_DIM = D_MODEL // B_GROUPS
EPS = 1e-6
LOG2E = 1.4426950408889634

HALO = 32
MXU_N = 256
LANES = 128
BF16_SUBLANES = 16
VMEM_LIMIT = 56 * 1024 * 1024

TM = 512


def _rmsnorm(x, g):
    ms = jnp.mean(x * x, axis=-1, keepdims=True)
    return x * lax.rsqrt(ms + EPS) * g


def _dot(a, b):
    return jnp.dot(a, b, preferred_element_type=jnp.float32)


def _resident(shape):
    return pl.BlockSpec(shape, lambda *_: (0,) * len(shape),
                        pipeline_mode=pl.Buffered(1))


def _cast_blocks(rows, n_steps):
    best = 1
    for nb in range(1, n_steps + 1):
        if rows % nb == 0 and (rows // nb) % BF16_SUBLANES == 0:
            best = nb
    return best


def _padded_cols(cols):
    return cols + LANES if (cols // LANES) % 8 == 0 else cols


def _cast_padded(w):
    pad = _padded_cols(w.shape[-1]) - w.shape[-1]
    wb = w.astype(jnp.bfloat16)
    if pad == 0:
        return wb
    return jnp.concatenate([wb, jnp.zeros((w.shape[0], pad), jnp.bfloat16)], axis=1)


def _cast_specs(casts, n_steps):
    in_specs, out_specs, out_shapes = [], [], []
    for w, layer in casts:
        _, rows, cols = w.shape
        nb = _cast_blocks(rows, n_steps)
        rb = rows // nb
        in_specs.append(pl.BlockSpec(
            (None, rb, cols),
            lambda i, layer=layer, nb=nb: (layer, jnp.minimum(i, nb - 1), 0)))
        out_specs.append(pl.BlockSpec(
            (rb, _padded_cols(cols)),
            lambda i, nb=nb: (jnp.minimum(i, nb - 1), 0)))
        out_shapes.append(
            jax.ShapeDtypeStruct((rows, _padded_cols(cols)), jnp.bfloat16))
    return in_specs, out_specs, out_shapes


def _run_casts(src_refs, dst_refs):
    for src, dst in zip(src_refs, dst_refs):
        rows, cols = src.shape
        dst[:, :cols] = src[...].astype(jnp.bfloat16)
        if dst.shape[1] > cols:
            dst[:, cols:] = jnp.zeros((rows, dst.shape[1] - cols), jnp.bfloat16)


FFN_SUBTILES = 2


def _ffn_kernel(*refs, n_cast, final_norm):
    x_ref, g_ref, win_ref, wout_ref, fg_ref = refs[:5]
    cast_src = refs[5:5 + n_cast]
    o_ref = refs[5 + n_cast]
    cast_dst = refs[6 + n_cast:6 + 2 * n_cast]
    xg_ref, a_ref, r_ref = refs[6 + 2 * n_cast:]
    _run_casts(cast_src, cast_dst)
    for r0 in range(0, x_ref.shape[0], TM):
        rows = pl.ds(r0, TM)
        x = x_ref[rows, :]
        xg_ref[rows, :] = (x * g_ref[...]).astype(jnp.bfloat16)
        r = lax.rsqrt(jnp.mean(x * x, axis=-1, keepdims=True) + EPS)
        r_ref[rows, :] = jnp.broadcast_to(-LOG2E * r, (TM, MXU_N))
        for c in range(D_FF // MXU_N):
            lo = c * MXU_N
            g_raw = _dot(xg_ref[rows, :], win_ref[:, lo:lo + MXU_N])
            u_raw = _dot(xg_ref[rows, :], win_ref[:, D_FF + lo:D_FF + lo + MXU_N])
            sig = 1.0 / (1.0 + jnp.exp2(g_raw * r_ref[rows, :]))
            a_ref[rows, lo:lo + MXU_N] = (g_raw * u_raw * sig).astype(jnp.bfloat16)
        row_blocks = 2 if final_norm else 1
        height = TM // row_blocks
        for q in range(row_blocks):
            qrows = pl.ds(r0 + q * height, height)
            rl = r_ref[qrows, :]
            half_r2 = jnp.concatenate(
                [(0.5 / (LOG2E * LOG2E)) * (rl * rl)] * (D_MODEL // MXU_N), axis=1)
            y = x_ref[qrows, :] + half_r2 * _dot(a_ref[qrows, :],
                                                 wout_ref[:, :D_MODEL])
            if final_norm:
                y = _rmsnorm(y, fg_ref[...])
            o_ref[qrows, :] = y


def _ffn(x2, g, w_in, w_out, fg, casts, *, final_norm):
    t, d = x2.shape
    tb = FFN_SUBTILES * TM
    n_steps = t // tb
    c_in, c_out, c_shapes = _cast_specs(casts, n_steps)
    kern = functools.partial(_ffn_kernel, n_cast=len(casts), final_norm=final_norm)
    outs = pl.pallas_call(
        kern,
        grid=(n_steps,),
        in_specs=[
            pl.BlockSpec((tb, d), lambda i: (i, 0)),
            _resident((1, d)),
            _resident(w_in.shape),
            _resident(w_out.shape),
            _resident((1, d)),
        ] + c_in,
        out_specs=[pl.BlockSpec((tb, d), lambda i: (i, 0))] + c_out,
        out_shape=[jax.ShapeDtypeStruct((t, d), jnp.float32)] + c_shapes,
        scratch_shapes=[
            pltpu.VMEM((tb, d), jnp.bfloat16),
            pltpu.VMEM((tb, D_FF), jnp.bfloat16),
            pltpu.VMEM((tb, MXU_N), jnp.float32),
        ],
        compiler_params=pltpu.CompilerParams(
            dimension_semantics=("arbitrary",), vmem_limit_bytes=VMEM_LIMIT),
        name="ffn",
    )(x2, g, w_in, w_out, fg, *[w for w, _ in casts])
    return outs[0], outs[1:]


def _gelu_tanh(x):
    k = -2.0 * 0.7978845608028654 * LOG2E
    return x / (1.0 + jnp.exp2(x * (k + (k * 0.044715) * (x * x))))


SGU_TM = 256
SGU_SUBTILES = 4
SGU_OUT_BLOCKS = (1, 4)


def _mixa_kernel(*refs, n_cast):
    (x_ref, g_ref, win_ref, vn_ref, ws_ref, bst_ref, wout_ref) = refs[:7]
    cast_src = refs[7:7 + n_cast]
    o_ref = refs[7 + n_cast]
    cast_dst = refs[8 + n_cast:8 + 2 * n_cast]
    h_ref, u_ref, v_ref, act_ref = refs[8 + 2 * n_cast:]
    _run_casts(cast_src, cast_dst)
    col_chunk = 2 * MXU_N
    n_chunks = A_DIM // col_chunk
    row_blocks, col_blocks = SGU_OUT_BLOCKS
    row = lax.broadcasted_iota(jnp.int32, (CHUNK, CHUNK), 0)
    col = lax.broadcasted_iota(jnp.int32, (CHUNK, CHUNK), 1)
    causal = col <= row

    def out_blocks(sub):
        def block(r, c):
            height, width = SGU_TM // row_blocks, D_MODEL // col_blocks
            rows = pl.ds(sub * SGU_TM + r * height, height)
            cols = slice(c * width, (c + 1) * width)
            o_ref[rows, cols] = x_ref[rows, cols] + _dot(
                act_ref[sub, r * height:(r + 1) * height, :], wout_ref[:, cols])
        return [functools.partial(block, r, c)
                for c in range(col_blocks) for r in range(row_blocks)]

    def activations(sub, fillers):
        rows = pl.ds(sub * SGU_TM, SGU_TM)
        fillers = list(fillers)
        n_after = -(-(len(fillers) - 1) // (n_chunks - 1)) if fillers else 0
        if fillers:
            fillers.pop(0)()
        h_ref[...] = _rmsnorm(x_ref[rows, :], g_ref[...]).astype(jnp.bfloat16)
        ssq = jnp.zeros((SGU_TM, 1), jnp.float32)
        for c in range(n_chunks):
            lo = c * col_chunk
            v = _gelu_tanh(
                _dot(h_ref[...], win_ref[:, A_DIM + lo:A_DIM + lo + col_chunk]))
            for _ in range(min(n_after, len(fillers))):
                fillers.pop(0)()
            ssq = ssq + jnp.sum(v * v, axis=-1, keepdims=True)
            v_ref[:, lo:lo + col_chunk] = v
        assert not fillers
        inv = lax.rsqrt(ssq * (1.0 / A_DIM) + EPS)

        def u_proj(hd):
            lo = hd * A_HEAD_DIM
            u_ref[:, lo:lo + A_HEAD_DIM] = _gelu_tanh(
                _dot(h_ref[...], win_ref[:, lo:lo + A_HEAD_DIM]))

        def gate_head(hd):
            lo = hd * A_HEAD_DIM
            ws = jnp.where(causal, ws_ref[hd], 0.0).astype(jnp.bfloat16)
            bias = bst_ref[:, hd:hd + 1]
            vn = (v_ref[:, lo:lo + A_HEAD_DIM] * inv
                  * vn_ref[:, lo:lo + A_HEAD_DIM]).astype(jnp.bfloat16)
            for ck in range(SGU_TM // CHUNK):
                r0 = ck * CHUNK
                gate = _dot(ws, vn[r0:r0 + CHUNK, :]) + bias
                act_ref[sub, r0:r0 + CHUNK, lo:lo + A_HEAD_DIM] = (
                    u_ref[r0:r0 + CHUNK, lo:lo + A_HEAD_DIM] * gate
                ).astype(jnp.bfloat16)

        u_proj(0)
        for hd in range(A_HEADS):
            if hd + 1 < A_HEADS:
                u_proj(hd + 1)
            gate_head(hd)

    pending = []
    for sub in range(SGU_SUBTILES):
        activations(sub, pending)
        pending = out_blocks(sub)
    for block in pending:
        block()


def _mixa(x2, g, w_in, v_norm, w_s, b_st, w_out, casts):
    t, d = x2.shape
    tb = SGU_SUBTILES * SGU_TM
    n_steps = t // tb
    c_in, c_out, c_shapes = _cast_specs(casts, n_steps)
    kern = functools.partial(_mixa_kernel, n_cast=len(casts))
    outs = pl.pallas_call(
        kern,
        grid=(n_steps,),
        in_specs=[
            pl.BlockSpec((tb, d), lambda i: (i, 0)),
            _resident((1, d)),
            _resident(w_in.shape),
            _resident((1, A_DIM)),
            _resident(w_s.shape),
            _resident(b_st.shape),
            _resident(w_out.shape),
        ] + c_in,
        out_specs=[pl.BlockSpec((tb, d), lambda i: (i, 0))] + c_out,
        out_shape=[jax.ShapeDtypeStruct((t, d), jnp.float32)] + c_shapes,
        scratch_shapes=[
            pltpu.VMEM((SGU_TM, d), jnp.bfloat16),
            pltpu.VMEM((SGU_TM, A_DIM), jnp.float32),
            pltpu.VMEM((SGU_TM, A_DIM), jnp.float32),
            pltpu.VMEM((SGU_SUBTILES, SGU_TM, A_DIM), jnp.bfloat16),
        ],
        compiler_params=pltpu.CompilerParams(
            dimension_semantics=("arbitrary",), vmem_limit_bytes=VMEM_LIMIT),
        name="mix_sgu",
    )(x2, g, w_in, v_norm, w_s, b_st, w_out, *[w for w, _ in casts])
    return outs[0], outs[1:]


POOL_SUBTILES = 2


def _mixb_kernel(*refs, n_cast, seq_steps):
    x_ref, g_ref, win_ref, wgrp_ref, scale_ref, wout_ref = refs[:6]
    cast_src = refs[6:6 + n_cast]
    o_ref = refs[6 + n_cast]
    cast_dst = refs[7 + n_cast:7 + 2 * n_cast]
    ext_ref, sum_a_ref, sum_b_ref, xg_ref, r_ref, act_ref = refs[7 + 2 * n_cast:]
    i = pl.program_id(0)
    rows_ext = HALO + TM
    last = POOL_SUBTILES - 1

    @pl.when(i == 0)
    def _():
        ext_ref[last, TM:rows_ext, :] = jnp.zeros((HALO, D_MODEL), jnp.float32)

    _run_casts(cast_src, cast_dst)
    step_in_seq = lax.rem(i, seq_steps)
    order = sorted(range(B_GROUPS), key=lambda gi: -POOL_WINDOWS[gi])
    halves = (order[:B_GROUPS // 2], order[B_GROUPS // 2:])

    def rows_of(sub):
        return pl.ds(sub * TM, TM)

    def in_proj(sub):
        x = x_ref[rows_of(sub), :]
        xg_ref[sub] = (x * g_ref[...]).astype(jnp.bfloat16)
        r = lax.rsqrt(jnp.mean(x * x, axis=-1, keepdims=True) + EPS)
        r_ref[sub] = jnp.broadcast_to(r, (TM, B_GROUP_DIM))
        carry = ext_ref[(sub - 1) % POOL_SUBTILES, TM:rows_ext, :]
        if sub == 0:
            carry = jnp.where(step_in_seq == 0, 0.0, carry)
        ext_ref[sub, 0:HALO, :] = carry
        for gi in order:
            cols = slice(gi * B_GROUP_DIM, (gi + 1) * B_GROUP_DIM)
            ext_ref[sub, HALO:, cols] = (
                _dot(xg_ref[sub], win_ref[:, cols]) * r_ref[sub])

    def pool_group(sub, gi):
        w = POOL_WINDOWS[gi]
        lo = gi * B_GROUP_DIM
        cols = slice(lo, lo + B_GROUP_DIM)
        pos = ((step_in_seq * POOL_SUBTILES + sub) * TM
               + lax.broadcasted_iota(jnp.int32, (TM, 1), 0))
        src, src_cols, level = ext_ref.at[sub], cols, 0
        k = 1
        while 2 * k < w:
            level += 1
            r0 = 8 * level
            dst = sum_a_ref if level % 2 else sum_b_ref
            dst[r0:rows_ext, :] = (src[r0:rows_ext, src_cols]
                                   + src[r0 - k:rows_ext - k, src_cols])
            src, src_cols = dst, slice(None)
            k *= 2
        acc = src[HALO:rows_ext, src_cols] + src[HALO - k:rows_ext - k, src_cols]
        cnt = jnp.minimum(pos + 1, w).astype(jnp.float32)
        pooled = acc / cnt - ext_ref[sub, HALO:, cols]
        y = _dot(pooled.astype(jnp.bfloat16),
                 wgrp_ref[lo:lo + B_GROUP_DIM, :B_GROUP_DIM])
        act_ref[sub, :, cols] = (y * scale_ref[:, cols]).astype(jnp.bfloat16)

    def half(sub, h):
        for gi in halves[h]:
            pool_group(sub, gi)
        lo = min(halves[h]) * B_GROUP_DIM
        hi = (max(halves[h]) + 1) * B_GROUP_DIM
        part = _dot(act_ref[sub, :, lo:hi], wout_ref[lo:hi, :D_MODEL])
        rows = rows_of(sub)
        if h == 0:
            o_ref[rows, :] = x_ref[rows, :] + part
        else:
            o_ref[rows, :] += part

    in_proj(0)
    for sub in range(POOL_SUBTILES):
        half(sub, 0)
        if sub + 1 < POOL_SUBTILES:
            in_proj(sub + 1)
        half(sub, 1)


def _mixb(x2, seq, g, w_in, w_grp, scale, w_out, casts):
    t, d = x2.shape
    tb = POOL_SUBTILES * TM
    n_steps = t // tb
    c_in, c_out, c_shapes = _cast_specs(casts, n_steps)
    kern = functools.partial(_mixb_kernel, n_cast=len(casts), seq_steps=seq // tb)
    outs = pl.pallas_call(
        kern,
        grid=(n_steps,),
        in_specs=[
            pl.BlockSpec((tb, d), lambda i: (i, 0)),
            _resident((1, d)),
            _resident(w_in.shape),
            _resident(w_grp.shape),
            _resident((1, d)),
            _resident(w_out.shape),
        ] + c_in,
        out_specs=[pl.BlockSpec((tb, d), lambda i: (i, 0))] + c_out,
        out_shape=[jax.ShapeDtypeStruct((t, d), jnp.float32)] + c_shapes,
        scratch_shapes=[
            pltpu.VMEM((POOL_SUBTILES, HALO + TM, d), jnp.float32),
            pltpu.VMEM((HALO + TM, B_GROUP_DIM), jnp.float32),
            pltpu.VMEM((HALO + TM, B_GROUP_DIM), jnp.float32),
            pltpu.VMEM((POOL_SUBTILES, TM, d), jnp.bfloat16),
            pltpu.VMEM((POOL_SUBTILES, TM, B_GROUP_DIM), jnp.float32),
            pltpu.VMEM((POOL_SUBTILES, TM, d), jnp.bfloat16),
        ],
        compiler_params=pltpu.CompilerParams(
            dimension_semantics=("arbitrary",), vmem_limit_bytes=VMEM_LIMIT),
        name="mix_pool",
    )(x2, g, w_in, w_grp, scale, w_out, *[w for w, _ in casts])
    return outs[0], outs[1:]


def _stack3(w):
    return w.reshape((-1,) + w.shape[-2:])


def kernel(x, ffn_norm, ffn_w_in, ffn_w_out, mix_norm, a_w_in, a_v_norm, a_w_s,
           a_b_s, a_w_out, b_w_in, b_w_grp, b_scale, b_w_out, final_norm):
    bsz, seq, d = x.shape
    depth = ffn_norm.shape[0]
    row = lambda v: v.reshape(1, -1)
    fw_in, fw_out = _stack3(ffn_w_in), _stack3(ffn_w_out)
    bw_grp = b_w_grp.reshape(b_w_grp.shape[0], -1, b_w_grp.shape[-1])

    subs = []
    for i in range(depth):
        subs.append(("ffn", i, 0, [(fw_in, 2 * i), (fw_out, 2 * i)]))
        j = i // 2
        if i % 2 == 0:
            subs.append(("sgu", i, j, [(a_w_in, j), (a_w_out, j)]))
        else:
            subs.append(("pool", i, j, [(b_w_in, j), (bw_grp, j), (b_w_out, j)]))
        subs.append(("ffn", i, 1, [(fw_in, 2 * i + 1), (fw_out, 2 * i + 1)]))

    x2 = x.reshape(bsz * seq, d)
    fg = row(final_norm)
    weights = [_cast_padded(w[l]) for w, l in subs[0][3]]
    for n, (kind, i, k, _) in enumerate(subs):
        casts = subs[n + 1][3] if n + 1 < len(subs) else []
        if kind == "ffn":
            x2, nxt = _ffn(x2, row(ffn_norm[i, k]), weights[0], weights[1], fg,
                           casts, final_norm=(n == len(subs) - 1))
        elif kind == "sgu":
            x2, nxt = _mixa(x2, row(mix_norm[i]), weights[0], row(a_v_norm[k]),
                            a_w_s[k], a_b_s[k].T, weights[1], casts)
        else:
            x2, nxt = _mixb(x2, seq, row(mix_norm[i]), weights[0], weights[1],
                            row(b_scale[k]), weights[2], casts)
        weights = list(nxt)
    return x2.reshape(bsz, seq, d)
```
